```python
import jax, jax.numpy as jnp
from jax import lax
import numpy as np

D_MODEL = 1024
BATCH = 1
SEQ = 16384
DEPTH = 2
DEC_BATCH = 32
DEC_SEQ = 4
PAST_LEN = 16384
PAGE_SIZE = 128

CONV_CH = D_MODEL // 4
CONV_WIDTH = 31
GLA_HEADS = 4
GLA_V = D_MODEL // 4
GLA_K = GLA_V // 2
GLA_DK = GLA_K // GLA_HEADS
GLA_DV = GLA_V // GLA_HEADS
GLA_LOWRANK = 16
GLA_TAU = 16.0
GLA_CHUNK = 64
POOL_CH = D_MODEL // 4
POOL_GROUPS = 4
POOL_GC = POOL_CH // POOL_GROUPS
POOL_WINDOWS = (2, 4, 8, 16)
POOL_STATE = 15
NSA_HEADS = 4
NSA_HD = 64
NSA_KV = 2
NSA_REP = NSA_HEADS // NSA_KV
CMP_LEN = 32
CMP_STRIDE = 16
SEL_BLOCK = 64
N_SEL = 16
WINDOW = 512
Q_BLOCK = 128
N_BRANCH = 4
D_FF = -((-8 * D_MODEL) // (3 * 256)) * 256
EPS = 1e-6
NEG = -1e30
FORCE = 1e4

kernel_name = 'hybrid_conv_gla_pool_nsa_decode_step'


def _in_sizes():
    return (2 * CONV_CH, GLA_K, GLA_K, GLA_V, GLA_V, GLA_LOWRANK, POOL_CH,
            NSA_HEADS * NSA_HD, 6 * NSA_KV * NSA_HD, 3 * NSA_HEADS, N_BRANCH * D_MODEL)


def _split_points():
    pts, acc = [], 0
    for s in _in_sizes()[:-1]:
        acc += s
        pts.append(acc)
    return pts


def rmsnorm(x, g):
    xf = x.astype(jnp.float32)
    y = xf * lax.rsqrt(jnp.mean(xf * xf, axis=-1, keepdims=True) + EPS)
    return (y * g.astype(jnp.float32)).astype(x.dtype)


def layernorm(x, g, b):
    xf = x.astype(jnp.float32)
    mu = jnp.mean(xf, axis=-1, keepdims=True)
    var = jnp.mean(jnp.square(xf - mu), axis=-1, keepdims=True)
    y = (xf - mu) * lax.rsqrt(var + EPS) * g.astype(jnp.float32) + b.astype(jnp.float32)
    return y.astype(x.dtype)


def masked_softmax(s, mask):
    s = jnp.where(mask, s, NEG)
    m = jnp.max(s, axis=-1, keepdims=True)
    p = jnp.where(mask, jnp.exp(s - m), 0.0)
    return p / jnp.maximum(jnp.sum(p, axis=-1, keepdims=True), 1e-30)


def conv_branch(a, prev, w, b, ln_g, ln_b):
    u = a[..., :CONV_CH] * jax.nn.sigmoid(a[..., CONV_CH:])
    u_ext = jnp.concatenate([prev.astype(u.dtype), u], axis=1)
    y = lax.conv_general_dilated(u_ext, w[:, None, :].astype(u.dtype), (1,), 'VALID',
                                 dimension_numbers=('NWC', 'WIO', 'NWC'),
                                 feature_group_count=CONV_CH) + b
    y = jax.nn.silu(layernorm(y, ln_g, ln_b))
    return y, u_ext[:, -(CONV_WIDTH - 1):]


def gla_chunk(S, q, k, v, la):
    C = q.shape[1]
    b = jnp.cumsum(la, axis=1)
    o_inter = jnp.einsum('bchk,bhkv->bchv', q * jnp.exp(b), S)
    causal = jnp.tril(jnp.ones((C, C), dtype=bool))
    diff = b[:, :, None] - b[:, None, :]
    decay = jnp.exp(jnp.where(causal[None, :, :, None, None], diff, -jnp.inf))
    A = jnp.einsum('bihk,bjhk,bijhk->bhij', q, k, decay)
    o = o_inter + jnp.einsum('bhij,bjhv->bihv', A, v)
    b_last = b[:, -1]
    k_dec = k * jnp.exp(b_last[:, None] - b)
    S_new = jnp.exp(b_last)[..., None] * S + jnp.einsum('bchk,bchv->bhkv', k_dec, v)
    return S_new, o


def gla_branch(q_in, k_in, v_in, og, lr, S0, w_a2, b_a, on_g):
    B, T, _ = q_in.shape
    f32 = jnp.float32
    q = q_in.reshape(B, T, GLA_HEADS, GLA_DK).astype(f32) * (GLA_DK ** -0.5)
    k = k_in.reshape(B, T, GLA_HEADS, GLA_DK).astype(f32)
    v = v_in.reshape(B, T, GLA_HEADS, GLA_DV).astype(f32)
    la = (jax.nn.log_sigmoid((lr @ w_a2 + b_a).astype(f32)) / GLA_TAU).reshape(B, T, GLA_HEADS, GLA_DK)
    cs = GLA_CHUNK if T % GLA_CHUNK == 0 else T
    n = T // cs

    def chunks(a):
        return a.reshape(B, n, cs, *a.shape[2:]).swapaxes(0, 1)

    S_fin, o = lax.scan(lambda S, c: gla_chunk(S, *c), S0.astype(f32),
                        (chunks(q), chunks(k), chunks(v), chunks(la)))
    o = o.swapaxes(0, 1).reshape(B, T, GLA_HEADS, GLA_DV)
    o = rmsnorm(o, on_g).reshape(B, T, GLA_V).astype(q_in.dtype) * jax.nn.silu(og)
    return o, S_fin


def pool_branch(u, prev, pos0, w_pool, scale):
    B, T, _ = u.shape
    P = POOL_STATE
    u_ext = jnp.concatenate([prev.astype(u.dtype), u], axis=1)
    uf = u_ext.astype(jnp.float32)
    cs = jnp.concatenate([jnp.zeros((B, 1, POOL_CH), jnp.float32), jnp.cumsum(uf, axis=1)], axis=1)
    pos = pos0 + jnp.arange(T)
    parts = []
    for gi, w in enumerate(POOL_WINDOWS):
        c0, c1 = gi * POOL_GC, (gi + 1) * POOL_GC
        win_sum = cs[:, P + 1:P + T + 1, c0:c1] - cs[:, P + 1 - w:P + T + 1 - w, c0:c1]
        cnt = jnp.minimum(pos + 1, w).astype(jnp.float32)[None, :, None]
        parts.append(win_sum / cnt - uf[:, P:, c0:c1])
    pooled = jnp.concatenate(parts, axis=-1).reshape(B, T, POOL_GROUPS, POOL_GC).astype(u.dtype)
    y = jnp.einsum('btgc,gcd->btgd', pooled, w_pool).reshape(B, T, POOL_CH) * scale
    return y, u_ext[:, -P:]


def compress(rows, pe, phi):
    B, Lp, G, D = rows.shape
    ch = rows.reshape(B, Lp // CMP_STRIDE, CMP_STRIDE, G, D)
    blk = jnp.concatenate([ch[:, :-1], ch[:, 1:]], axis=2) + pe[:, None, :]
    return jnp.einsum('bnlgd,lde->bnge', blk, phi)


def nsa_attend(q, pos_q, kc, vc, kb, vb, kw, vw, pos_w):
    B, Tq = q.shape[:2]
    f32 = jnp.float32
    qg = q.astype(f32).reshape(B, Tq, NSA_KV, NSA_REP, NSA_HD)
    nc = kc.shape[1]
    c_end = jnp.arange(nc) * CMP_STRIDE + (CMP_LEN - 1)
    s_c = jnp.einsum('btgrd,bngd->bgrtn', qg, kc.astype(f32))
    p_c = masked_softmax(s_c, c_end[None, :] <= pos_q[:, None])
    o_c = jnp.einsum('bgrtn,bngd->btgrd', p_c, vc.astype(f32))
    nb = kb.shape[1]
    ratio = SEL_BLOCK // CMP_STRIDE
    imp = jnp.pad(jnp.sum(p_c, axis=2), ((0, 0), (0, 0), (0, 0), (1, 1)))
    blk = imp[..., :ratio * nb].reshape(B, NSA_KV, Tq, nb, ratio).sum(-1) + imp[..., ratio::ratio]
    bidx = jnp.arange(nb)
    avail = bidx[None, :] * SEL_BLOCK <= pos_q[:, None]
    forced = (bidx[None, :] == (pos_q // SEL_BLOCK)[:, None]) | (bidx[None, :] == 0)
    score = jnp.where(avail, jnp.where(forced, FORCE, blk), NEG)
    top_s, sel = lax.top_k(score, min(N_SEL, nb))
    ok = top_s > 0.5 * NEG
    gather = jax.vmap(lambda blocks, idx: blocks[idx, :, jnp.arange(NSA_KV)[:, None, None], :])
    ks = gather(kb, sel).astype(f32)
    vs = gather(vb, sel).astype(f32)
    n = sel.shape[-1]
    kpos = sel[..., None] * SEL_BLOCK + jnp.arange(SEL_BLOCK)
    m_s = ok[..., None] & (kpos <= pos_q[:, None, None])
    s_s = jnp.einsum('btgrd,bgtnkd->bgrtnk', qg, ks).reshape(B, NSA_KV, NSA_REP, Tq, n * SEL_BLOCK)
    p_s = masked_softmax(s_s, m_s.reshape(B, NSA_KV, 1, Tq, n * SEL_BLOCK))
    o_s = jnp.einsum('bgrtm,bgtmd->btgrd', p_s, vs.reshape(B, NSA_KV, Tq, n * SEL_BLOCK, NSA_HD))
    s_w = jnp.einsum('btgrd,bkgd->bgrtk', qg, kw.astype(f32))
    dpos = pos_q[:, None] - pos_w[None, :]
    m_w = (dpos >= 0) & (dpos < WINDOW) & (pos_w[None, :] >= 0)
    p_w = masked_softmax(s_w, m_w)
    o_w = jnp.einsum('bgrtk,bkgd->btgrd', p_w, vw.astype(f32))
    return jnp.stack([o_c, o_s, o_w]).reshape(3, B, Tq, NSA_HEADS, NSA_HD)


def nsa_branch(q_raw, kv_raw, g_raw, past4, win_prev, pos0, lw):
    B, T, _ = q_raw.shape
    kn = lw['nsa_kn']
    q = rmsnorm(q_raw.reshape(B, T, NSA_HEADS, NSA_HD), lw['nsa_qn']) * (NSA_HD ** -0.5)
    kv = kv_raw.reshape(B, T, 6, NSA_KV, NSA_HD)
    ks = rmsnorm(kv[:, :, 2], kn[1])
    kw = rmsnorm(kv[:, :, 4], kn[2])
    rows4 = jnp.stack([kv[:, :, 0], kv[:, :, 1], ks, kv[:, :, 3]], axis=2)
    win_new = jnp.stack([kw, kv[:, :, 5]], axis=2)
    if past4 is None:
        full4, win_all = rows4, win_new
    else:
        full4 = jnp.concatenate([past4.astype(rows4.dtype), rows4], axis=1)
        win_all = jnp.concatenate([win_prev.astype(win_new.dtype), win_new], axis=1)
    L = full4.shape[1]
    Lp = -(-L // SEL_BLOCK) * SEL_BLOCK
    full4 = jnp.pad(full4, ((0, 0), (0, Lp - L), (0, 0), (0, 0), (0, 0)))
    kc = rmsnorm(compress(full4[:, :, 0], lw['nsa_pe'][0], lw['nsa_phi'][0]), kn[0])
    vc = compress(full4[:, :, 1], lw['nsa_pe'][1], lw['nsa_phi'][1])
    nb = Lp // SEL_BLOCK
    kb = full4[:, :, 2].reshape(B, nb, SEL_BLOCK, NSA_KV, NSA_HD)
    vb = full4[:, :, 3].reshape(B, nb, SEL_BLOCK, NSA_KV, NSA_HD)
    if past4 is None:
        w_pad = jnp.pad(win_all, ((0, 0), (WINDOW, 0), (0, 0), (0, 0), (0, 0)))

        def one_block(i):
            s0 = i * Q_BLOCK
            qb = lax.dynamic_slice_in_dim(q, s0, Q_BLOCK, axis=1)
            wb = lax.dynamic_slice_in_dim(w_pad, s0, WINDOW + Q_BLOCK, axis=1)
            pos_q = s0 + jnp.arange(Q_BLOCK)
            pos_w = s0 - WINDOW + jnp.arange(WINDOW + Q_BLOCK)
            return nsa_attend(qb, pos_q, kc, vc, kb, vb, wb[:, :, 0], wb[:, :, 1], pos_w)

        o = lax.map(one_block, jnp.arange(T // Q_BLOCK))
        o = jnp.transpose(o, (1, 2, 0, 3, 4, 5)).reshape(3, B, T, NSA_HEADS, NSA_HD)
    else:
        pos_q = pos0 + jnp.arange(T)
        pos_w = (pos0 - win_prev.shape[1]) + jnp.arange(win_all.shape[1])
        o = nsa_attend(q, pos_q, kc, vc, kb, vb, win_all[:, :, 0], win_all[:, :, 1], pos_w)
    g = jax.nn.sigmoid((g_raw + lw['nsa_gb']).astype(jnp.float32)).reshape(B, T, NSA_HEADS, 3)
    y = jnp.einsum('bthc,cbthd->bthd', g, o).reshape(B, T, NSA_HEADS * NSA_HD).astype(q_raw.dtype)
    keep = min(WINDOW, win_all.shape[1])
    return y, rows4, win_all[:, -keep:]


def decoder_layer(x, lw, conv_prev, pool_prev, gla_prev, past4, win_prev, pos0):
    B, T, _ = x.shape
    h = rmsnorm(x, lw['norm1'])
    proj = h @ lw['w_in']
    (a_conv, g_q, g_k, g_v, g_o, g_lr, u_pool, n_q, n_kv, n_g, gate_raw) = jnp.split(proj, _split_points(), axis=-1)
    ya, conv_state = conv_branch(a_conv, conv_prev, lw['conv_w'], lw['conv_b'], lw['conv_ln_g'], lw['conv_ln_b'])
    yb, gla_state = gla_branch(g_q, g_k, g_v, g_o, g_lr, gla_prev, lw['gla_wa2'], lw['gla_ba'], lw['gla_onorm'])
    yc, pool_state = pool_branch(u_pool, pool_prev, pos0, lw['pool_w'], lw['pool_scale'])
    yd, rows4, win_state = nsa_branch(n_q, n_kv, n_g, past4, win_prev, pos0, lw)
    branches = jnp.stack([ya @ lw['w_out_conv'], yb @ lw['w_out_gla'],
                          yc @ lw['w_out_pool'], yd @ lw['w_out_nsa']], axis=2)
    gates = jax.nn.sigmoid(gate_raw.reshape(B, T, N_BRANCH, D_MODEL))
    x = x + jnp.sum(gates * branches, axis=2) @ lw['w_o']
    h2 = rmsnorm(x, lw['norm2'])
    x = x + (jax.nn.silu(h2 @ lw['ffn_gate']) * (h2 @ lw['ffn_up'])) @ lw['ffn_down']
    return x, rows4, win_state, conv_state, pool_state, gla_state


def setup_inputs(seed: int = 0) -> dict:
    key = jax.random.key(seed)
    keys = list(jax.random.split(key, 48))
    f32 = jnp.float32

    def nrm(shape, scale):
        return jax.random.normal(keys.pop(), shape, f32) * scale

    def gain(shape):
        return 1.0 + nrm(shape, 0.02)

    n_pages = PAST_LEN // PAGE_SIZE
    n_used = DEC_BATCH * n_pages
    n_pool = n_used + max(1, n_used // 4)
    wbuf = min(WINDOW, PAST_LEN)
    in_cols = sum(_in_sizes())
    page_table = jax.random.permutation(keys.pop(), n_pool)[:n_used].reshape(DEC_BATCH, n_pages).astype(jnp.int32)
    return {
        'x_prompt': nrm((BATCH, SEQ, D_MODEL), 1.0),
        'x_sample': nrm((DEC_BATCH, DEC_SEQ, D_MODEL), 1.0),
        'cache_nsa_kv': nrm((DEPTH, n_pool, PAGE_SIZE, 4, NSA_KV, NSA_HD), 1.0),
        'cache_win_kv': nrm((DEPTH, DEC_BATCH, wbuf, 2, NSA_KV, NSA_HD), 1.0),
        'state_conv': nrm((DEPTH, DEC_BATCH, CONV_WIDTH - 1, CONV_CH), 0.5),
        'state_pool': nrm((DEPTH, DEC_BATCH, POOL_STATE, POOL_CH), 1.0),
        'state_gla': nrm((DEPTH, DEC_BATCH, GLA_HEADS, GLA_DK, GLA_DV), 0.5),
        'page_table': page_table,
        'norm1': gain((DEPTH, D_MODEL)),
        'w_in': nrm((DEPTH, D_MODEL, in_cols), D_MODEL ** -0.5),
        'conv_w': nrm((DEPTH, CONV_WIDTH, CONV_CH), CONV_WIDTH ** -0.5),
        'conv_b': nrm((DEPTH, CONV_CH), 0.02),
        'conv_ln_g': gain((DEPTH, CONV_CH)),
        'conv_ln_b': nrm((DEPTH, CONV_CH), 0.02),
        'w_out_conv': nrm((DEPTH, CONV_CH, D_MODEL), CONV_CH ** -0.5),
        'gla_wa2': nrm((DEPTH, GLA_LOWRANK, GLA_K), GLA_LOWRANK ** -0.5),
        'gla_ba': nrm((DEPTH, GLA_K), 0.02),
        'gla_onorm': gain((DEPTH, GLA_DV)),
        'w_out_gla': nrm((DEPTH, GLA_V, D_MODEL), GLA_V ** -0.5),
        'pool_w': nrm((DEPTH, POOL_GROUPS, POOL_GC, POOL_GC), POOL_GC ** -0.5),
        'pool_scale': gain((DEPTH, POOL_CH)),
        'w_out_pool': nrm((DEPTH, POOL_CH, D_MODEL), POOL_CH ** -0.5),
        'nsa_qn': gain((DEPTH, NSA_HD)),
        'nsa_kn': gain((DEPTH, 3, NSA_HD)),
        'nsa_pe': nrm((DEPTH, 2, CMP_LEN, NSA_HD), 0.02),
        'nsa_phi': nrm((DEPTH, 2, CMP_LEN, NSA_HD, NSA_HD), (CMP_LEN * NSA_HD) ** -0.5),
        'nsa_gb': nrm((DEPTH, 3 * NSA_HEADS), 0.02),
        'w_out_nsa': nrm((DEPTH, NSA_HEADS * NSA_HD, D_MODEL), (NSA_HEADS * NSA_HD) ** -0.5),
        'w_o': nrm((DEPTH, D_MODEL, D_MODEL), D_MODEL ** -0.5),
        'norm2': gain((DEPTH, D_MODEL)),
        'ffn_gate': nrm((DEPTH, D_MODEL, D_FF), D_MODEL ** -0.5),
        'ffn_up': nrm((DEPTH, D_MODEL, D_FF), D_MODEL ** -0.5),
        'ffn_down': nrm((DEPTH, D_FF, D_MODEL), D_FF ** -0.5),
    }


def reference(x_prompt, x_sample, cache_nsa_kv, cache_win_kv, state_conv, state_pool, state_gla, page_table,
              norm1, w_in, conv_w, conv_b, conv_ln_g, conv_ln_b, w_out_conv, gla_wa2, gla_ba, gla_onorm, w_out_gla,
              pool_w, pool_scale, w_out_pool, nsa_qn, nsa_kn, nsa_pe, nsa_phi, nsa_gb, w_out_nsa, w_o, norm2,
              ffn_gate, ffn_up, ffn_down):
    xp, xs = x_prompt, x_sample
    B, Bd = xp.shape[0], xs.shape[0]
    past_len = page_table.shape[1] * cache_nsa_kv.shape[2]
    p_nsa, p_win, p_conv, p_pool, p_gla = [], [], [], [], []
    s_nsa, s_win, s_conv, s_pool, s_gla = [], [], [], [], []
    for l in range(DEPTH):
        lw = {'norm1': norm1[l], 'w_in': w_in[l], 'conv_w': conv_w[l], 'conv_b': conv_b[l],
              'conv_ln_g': conv_ln_g[l], 'conv_ln_b': conv_ln_b[l], 'w_out_conv': w_out_conv[l],
              'gla_wa2': gla_wa2[l], 'gla_ba': gla_ba[l], 'gla_onorm': gla_onorm[l], 'w_out_gla': w_out_gla[l],
              'pool_w': pool_w[l], 'pool_scale': pool_scale[l], 'w_out_pool': w_out_pool[l],
              'nsa_qn': nsa_qn[l], 'nsa_kn': nsa_kn[l], 'nsa_pe': nsa_pe[l], 'nsa_phi': nsa_phi[l],
              'nsa_gb': nsa_gb[l], 'w_out_nsa': w_out_nsa[l], 'w_o': w_o[l], 'norm2': norm2[l],
              'ffn_gate': ffn_gate[l], 'ffn_up': ffn_up[l], 'ffn_down': ffn_down[l]}
        xp, r4, wn, cv, pl, gl = decoder_layer(
            xp, lw,
            jnp.zeros((B, CONV_WIDTH - 1, CONV_CH), xp.dtype),
            jnp.zeros((B, POOL_STATE, POOL_CH), xp.dtype),
            jnp.zeros((B, GLA_HEADS, GLA_DK, GLA_DV), jnp.float32),
            None, None, 0)
        p_nsa.append(r4); p_win.append(wn); p_conv.append(cv); p_pool.append(pl); p_gla.append(gl)
        past4 = cache_nsa_kv[l][page_table].reshape(Bd, past_len, 4, NSA_KV, NSA_HD)
        xs, r4, wn, cv, pl, gl = decoder_layer(
            xs, lw, state_conv[l], state_pool[l], state_gla[l], past4, cache_win_kv[l], past_len)
        s_nsa.append(r4); s_win.append(wn); s_conv.append(cv); s_pool.append(pl); s_gla.append(gl)
    return (xp, xs,
            jnp.stack(p_nsa), jnp.stack(p_win), jnp.stack(p_conv), jnp.stack(p_pool), jnp.stack(p_gla),
            jnp.stack(s_nsa), jnp.stack(s_win), jnp.stack(s_conv), jnp.stack(s_pool), jnp.stack(s_gla))
```

```python
import functools

import jax
import jax.numpy as jnp
from jax import lax
from jax.experimental import pallas as pl
from jax.experimental.pallas import tpu as pltpu

F32, BF16 = jnp.float32, jnp.bfloat16

D_MODEL = 1024
CONV_CH = 256
CONV_WIDTH = 31
GLA_HEADS, GLA_DK, GLA_DV = 4, 32, 64
GLA_LOWRANK = 16
GLA_TAU = 16.0
POOL_CH = 256
POOL_WINDOWS = (2, 4, 8, 16)
POOL_STATE = 15
NSA_HEADS, NSA_HD, NSA_KV = 4, 64, 2
CMP_STRIDE = 16
SEL_BLOCK = 64
N_SEL = 16
WINDOW = 512
Q_BLOCK = 128
D_FF = 2816
EPS = 1e-6
NEG = -1e30
FORCE = 1e4

LANES = 128
SUBLANES = 8
VMEM_LIMIT = 56 * 1024 * 1024

GLA_SUB = 16
SEL_TILE = 1024
CMP_PAGES = 16

_P_A, _P_QK, _P_V, _P_O, _P_LR, _P_UP, _P_NQ, _P_KV, _P_NG, _P_END = (
    0, 512, 768, 1024, 1280, 1408, 1664, 1920, 2688, 2816)


def _bdot(a, b):
    return jnp.dot(a.astype(BF16), b.astype(BF16), preferred_element_type=F32)


def _nt(a, b):
    return lax.dot_general(a, b, (((1,), (1,)), ((), ())), preferred_element_type=F32)


def _tn(a, b):
    return lax.dot_general(a, b, (((0,), (0,)), ((), ())), preferred_element_type=F32)


def _split(x):
    hi = x.astype(BF16)
    return hi, (x - hi.astype(F32)).astype(BF16)


def _rms(x, g):
    return x * lax.rsqrt(jnp.mean(x * x, axis=-1, keepdims=True) + EPS) * g


def _segmean(x2, e_ref):
    hi, lo = _split(x2)
    e = e_ref[...]
    return jnp.dot(hi, e, preferred_element_type=F32) + jnp.dot(lo, e, preferred_element_type=F32)


def _const_spec(shape):
    nd = len(shape)
    return pl.BlockSpec(shape, lambda *_: (0,) * nd)


def _params(sem):
    return pltpu.CompilerParams(dimension_semantics=sem, vmem_limit_bytes=VMEM_LIMIT)


def _proj_kernel(x_ref, g1_ref, wp_ref, wa2_ref, ba_ref, qn_ref, kn1_ref, kn2_ref, gb_ref, e256_ref, e128_ref,
                 a_ref, qk_ref, v_ref, o_ref, la_ref, up_ref, q_ref, rows_ref, win_ref, ng_ref, *t_refs):
    h = _rms(x_ref[...], g1_ref[...]).astype(BF16)

    def seg(a, b):
        return jnp.dot(h, wp_ref[:, a:b], preferred_element_type=F32)

    a_ref[...] = seg(_P_A, _P_QK)
    qk_ref[...] = seg(_P_QK, _P_V)
    v_ref[...] = seg(_P_V, _P_O)
    o_ref[...] = seg(_P_O, _P_LR)
    z = _bdot(seg(_P_LR, _P_UP), wa2_ref[...]) + ba_ref[...]
    la_ref[...] = (jnp.minimum(z, 0.0) - jnp.log(1.0 + jnp.exp(-jnp.abs(z)))) * (1.0 / GLA_TAU)
    up_ref[...] = seg(_P_UP, _P_NQ)
    qr = seg(_P_NQ, _P_KV)
    q_ref[...] = qr * lax.rsqrt(_segmean(qr * qr, e256_ref) + EPS) * qn_ref[...]
    kv = seg(_P_KV, _P_NG)
    k_sel = kv[:, 256:384]
    k_sel = k_sel * lax.rsqrt(_segmean(k_sel * k_sel, e128_ref) + EPS) * kn1_ref[...]
    k_win = kv[:, 512:640]
    k_win = k_win * lax.rsqrt(_segmean(k_win * k_win, e128_ref) + EPS) * kn2_ref[...]
    v_sel, v_win = kv[:, 384:512], kv[:, 640:768]
    rows_ref[:, 0:256] = kv[:, 0:256]
    rows_ref[:, 256:384] = k_sel
    rows_ref[:, 384:512] = v_sel
    win_ref[:, 0:128] = k_win
    win_ref[:, 128:256] = v_win
    ng_ref[...] = jax.nn.sigmoid(seg(_P_NG, _P_END) + gb_ref[...])
    if t_refs:
        ksb_ref, svt_ref, kwb_ref, wvt_ref = t_refs
        ksb_ref[...] = k_sel.astype(BF16)
        svt_ref[...] = v_sel.T.astype(BF16)
        kwb_ref[...] = k_win.astype(BF16)
        wvt_ref[...] = v_win.T.astype(BF16)


def _proj_call(x, lw, with_t):
    n = x.shape[0]
    tm = min(512, n)
    row = lambda w: pl.BlockSpec((tm, w), lambda i: (i, 0))
    consts = (lw['g1'], lw['wp'], lw['wa2'], lw['ba'], lw['qn'], lw['kn1'], lw['kn2'], lw['gb'], lw['e256'], lw['e128'])
    widths = (512, 256, 256, 256, 128, 256, 256, 512, 256, 128)
    out_shape = [jax.ShapeDtypeStruct((n, w), F32) for w in widths]
    out_specs = [row(w) for w in widths]
    if with_t:
        col = pl.BlockSpec((LANES, tm), lambda i: (0, i))
        out_shape += [jax.ShapeDtypeStruct((n, LANES), BF16), jax.ShapeDtypeStruct((LANES, n), BF16)] * 2
        out_specs += [row(LANES), col] * 2
    return pl.pallas_call(
        _proj_kernel, grid=(n // tm,),
        in_specs=[row(D_MODEL)] + [_const_spec(c.shape) for c in consts],
        out_specs=out_specs, out_shape=out_shape,
        compiler_params=_params(("parallel",)), name="proj",
    )(x, *consts)


_CONV_PAD = 32


def _conv_kernel(a_ref, prev_ref, w_ref, b_ref, lg_ref, lb_ref, y_ref, u_ref, ext_ref, *, tm):
    @pl.when(pl.program_id(1) == 0)
    def _():
        ext_ref[0:_CONV_PAD, :] = prev_ref[0]

    a = a_ref[0]
    u = a[:, :CONV_CH] * jax.nn.sigmoid(a[:, CONV_CH:])
    ext_ref[_CONV_PAD:_CONV_PAD + tm, :] = u
    acc = jnp.zeros((tm, CONV_CH), F32) + b_ref[...]
    first = _CONV_PAD - (CONV_WIDTH - 1)
    for j in range(CONV_WIDTH):
        acc = acc + ext_ref[first + j:first + j + tm, :] * w_ref[j:j + 1, :]
    mu = jnp.mean(acc, axis=-1, keepdims=True)
    cen = acc - mu
    var = jnp.mean(cen * cen, axis=-1, keepdims=True)
    yn = cen * lax.rsqrt(var + EPS) * lg_ref[...] + lb_ref[...]
    y_ref[0] = yn * jax.nn.sigmoid(yn)
    u_ref[0] = u
    ext_ref[0:_CONV_PAD, :] = ext_ref[tm:tm + _CONV_PAD, :]


def _conv_call(a, prev, lw):
    b, t, _ = a.shape
    tm = min(512, t)
    prev = jnp.pad(prev, ((0, 0), (_CONV_PAD - (CONV_WIDTH - 1), 0), (0, 0)))
    blk = lambda w: pl.BlockSpec((1, tm, w), lambda i, j: (i, j, 0))
    consts = (lw['cw'], lw['cb'], lw['clg'], lw['clb'])
    return pl.pallas_call(
        functools.partial(_conv_kernel, tm=tm), grid=(b, t // tm),
        in_specs=[blk(2 * CONV_CH), pl.BlockSpec((1, _CONV_PAD, CONV_CH), lambda i, j: (i, 0, 0))]
        + [_const_spec(c.shape) for c in consts],
        out_specs=[blk(CONV_CH), blk(CONV_CH)],
        out_shape=[jax.ShapeDtypeStruct((b, t, CONV_CH), F32)] * 2,
        scratch_shapes=[pltpu.VMEM((_CONV_PAD + tm, CONV_CH), F32)],
        compiler_params=_params(("arbitrary", "arbitrary")), name="conv",
    )(a, prev, *consts)


_POOL_PAD = 16


def _pool_kernel(u_ref, prev_ref, w_ref, sc_ref, y_ref, ext_ref, *, tm, pos0):
    t = pl.program_id(1)

    @pl.when(t == 0)
    def _():
        ext_ref[0:_POOL_PAD, :] = prev_ref[0]

    u = u_ref[0]
    ext_ref[_POOL_PAD:_POOL_PAD + tm, :] = u

    def back(d):
        return ext_ref[_POOL_PAD - d:_POOL_PAD - d + tm, :]

    sums, acc, d = [], u, 1
    for w in POOL_WINDOWS:
        while d < w:
            acc = acc + back(d)
            d += 1
        sums.append(acc)
    grp = lax.broadcasted_iota(jnp.int32, (tm, POOL_CH), 1) // (POOL_CH // len(POOL_WINDOWS))
    pos = pos0 + t * tm + lax.broadcasted_iota(jnp.int32, (tm, POOL_CH), 0)
    win, width = sums[-1], jnp.full((tm, POOL_CH), POOL_WINDOWS[-1], jnp.int32)
    for gi in range(len(POOL_WINDOWS) - 2, -1, -1):
        win = jnp.where(grp == gi, sums[gi], win)
        width = jnp.where(grp == gi, POOL_WINDOWS[gi], width)
    cnt = jnp.minimum(pos + 1, width).astype(F32)
    pooled = win / cnt - u
    y_ref[0] = _bdot(pooled, w_ref[...]) * sc_ref[...]
    ext_ref[0:_POOL_PAD, :] = ext_ref[tm:tm + _POOL_PAD, :]


def _pool_call(u, prev, pos0, lw):
    b, t, _ = u.shape
    tm = min(512, t)
    prev = jnp.pad(prev, ((0, 0), (_POOL_PAD - POOL_STATE, 0), (0, 0)))
    blk = pl.BlockSpec((1, tm, POOL_CH), lambda i, j: (i, j, 0))
    return pl.pallas_call(
        functools.partial(_pool_kernel, tm=tm, pos0=pos0), grid=(b, t // tm),
        in_specs=[blk, pl.BlockSpec((1, _POOL_PAD, POOL_CH), lambda i, j: (i, 0, 0)),
                  _const_spec(lw['pw'].shape), _const_spec(lw['psc'].shape)],
        out_specs=blk, out_shape=jax.ShapeDtypeStruct((b, t, POOL_CH), F32),
        scratch_shapes=[pltpu.VMEM((_POOL_PAD + tm, POOL_CH), F32)],
        compiler_params=_params(("arbitrary", "arbitrary")), name="pool",
    )(u, prev, lw['pw'], lw['psc'])


def _gla_kernel(qk_ref, v_ref, la_ref, og_ref, s0_ref, on_ref, eh_ref, mbd_ref, e256_ref,
                y_ref, sfin_ref, st_ref, kext, bext, vext, qs_s, ks_s, g_s, o_s, *, tm):
    R = GLA_SUB
    t = pl.program_id(1)
    kw = GLA_HEADS * GLA_DK

    @pl.when(t == 0)
    def _():
        st_ref[...] = s0_ref[0]
        kext[0:R, :] = jnp.zeros((R, kw), F32)
        bext[0:R, :] = jnp.zeros((R, kw), F32)
        vext[0:R, :] = jnp.zeros((R, GLA_HEADS * GLA_DV), F32)

    qk = qk_ref[0]
    q = qk[:, :kw] * (GLA_DK ** -0.5)
    k = qk[:, kw:]
    v = v_ref[0]
    la = la_ref[0]
    r = lax.broadcasted_iota(jnp.int32, (tm, kw), 0) % R
    b = la
    c = la
    for s in (1, 2, 4, 8):
        b = b + jnp.where(r >= s, pltpu.roll(b, s, axis=0), 0.0)
        c = c + jnp.where(r < R - s, pltpu.roll(c, tm - s, axis=0), 0.0)
    kext[R:R + tm, :] = k
    bext[R:R + tm, :] = b
    vext[R:R + tm, :] = v
    o = jnp.zeros((tm, GLA_HEADS * GLA_DV), F32)
    for d in range(R):
        kd = kext[R - d:R - d + tm, :]
        bd = bext[R - d:R - d + tm, :]
        vd = vext[R - d:R - d + tm, :]
        e = jnp.exp(jnp.where(r >= d, b - bd, NEG))
        a = jnp.dot((q * kd * e).astype(BF16), eh_ref[...], preferred_element_type=F32)
        o = o + a * vd
    o_s[...] = o
    qs_s[...] = q * jnp.exp(b)
    ks_s[...] = k * jnp.exp(c - la)
    g_s[...] = jnp.exp(b + c - la)

    def body(i, carry):
        r0 = pl.multiple_of(i * R, R)
        st = st_ref[...]
        o_s[pl.ds(r0, R), :] += _nt(qs_s[pl.ds(r0, R), :].astype(BF16), st.astype(BF16))
        upd = _tn(vext[pl.ds(R + r0, R), :].astype(BF16), ks_s[pl.ds(r0, R), :].astype(BF16))
        st_ref[...] = st * g_s[pl.ds(r0, 1), :] + upd * mbd_ref[...]
        return carry

    lax.fori_loop(0, tm // R, body, 0)
    o = o_s[...]
    on = o * lax.rsqrt(_segmean(o * o, e256_ref) + EPS) * on_ref[...]
    og = og_ref[0]
    y_ref[0] = on * (og * jax.nn.sigmoid(og))

    @pl.when(t == pl.num_programs(1) - 1)
    def _():
        sfin_ref[0] = st_ref[...]


def _gla_call(qk, v, la, og, s0, lw):
    b, t, _ = qk.shape
    tp = -(-t // GLA_SUB) * GLA_SUB
    if tp != t:
        pad = lambda a: jnp.pad(a, ((0, 0), (0, tp - t), (0, 0)))
        qk, v, la, og = pad(qk), pad(v), pad(la), pad(og)
    tm = min(256, tp)
    kw, vw = GLA_HEADS * GLA_DK, GLA_HEADS * GLA_DV
    eye = jnp.eye(GLA_HEADS, dtype=F32)
    st0 = jnp.einsum('bhkv,hg->bhvgk', s0.astype(F32), eye).reshape(b, vw, kw)
    blk = lambda w: pl.BlockSpec((1, tm, w), lambda i, j: (i, j, 0))
    st_spec = pl.BlockSpec((1, vw, kw), lambda i, j: (i, 0, 0))
    consts = (lw['on'], lw['eh'], lw['mbd'], lw['e256'])
    y, st = pl.pallas_call(
        functools.partial(_gla_kernel, tm=tm), grid=(b, tp // tm),
        in_specs=[blk(2 * kw), blk(vw), blk(kw), blk(vw), st_spec] + [_const_spec(c.shape) for c in consts],
        out_specs=[blk(vw), st_spec],
        out_shape=[jax.ShapeDtypeStruct((b, tp, vw), F32), jax.ShapeDtypeStruct((b, vw, kw), F32)],
        scratch_shapes=[pltpu.VMEM((vw, kw), F32),
                        pltpu.VMEM((GLA_SUB + tm, kw), F32), pltpu.VMEM((GLA_SUB + tm, kw), F32),
                        pltpu.VMEM((GLA_SUB + tm, vw), F32),
                        pltpu.VMEM((tm, kw), F32), pltpu.VMEM((tm, kw), F32), pltpu.VMEM((tm, kw), F32),
                        pltpu.VMEM((tm, vw), F32)],
        compiler_params=_params(("arbitrary", "arbitrary")), name="gla",
    )(qk, v, la, og, st0, *consts)
    st = st.reshape(b, GLA_HEADS, GLA_DV, GLA_HEADS, GLA_DK)
    s_fin = jnp.stack([st[:, h, :, h, :] for h in range(GLA_HEADS)], axis=1).swapaxes(-1, -2)
    return y[:, :t], s_fin


def _cmp_copy(pt_ref, cache_ref, buf, sem, step, slot, j, kind, n_steps, page):
    bb, ss = step // n_steps, step % n_steps
    pg = pt_ref[bb, ss * CMP_PAGES + j]
    return pltpu.make_async_copy(cache_ref.at[pg, :, pl.ds(kind * LANES, LANES)],
                                 buf.at[slot, kind, pl.ds(j * page, page), :], sem.at[slot])


def _compress_kernel(pt_ref, cache_ref, wk_ref, wv_ref, pek_ref, pev_ref, kn0_ref, e128_ref,
                     kch_ref, kcl_ref, vct_ref, buf, sem, f_s, bias_s, *, n_steps, page, total):
    bi, si = pl.program_id(0), pl.program_id(1)
    step = bi * n_steps + si
    slot = step % 2
    rows = CMP_PAGES * page
    nchunk = rows // CMP_STRIDE

    def start(st, sl):
        for j in range(CMP_PAGES):
            for kind in range(2):
                _cmp_copy(pt_ref, cache_ref, buf, sem, st, sl, j, kind, n_steps, page).start()

    @pl.when(step == 0)
    def _():
        start(step, slot)
        rid = lax.broadcasted_iota(jnp.int32, (CMP_STRIDE, 2 * LANES), 0)
        bk = jnp.zeros((CMP_STRIDE, 2 * LANES), F32)
        bv = jnp.zeros((CMP_STRIDE, 2 * LANES), F32)
        for l in range(CMP_STRIDE):
            pk, pv = pek_ref[...], pev_ref[...]
            tk = jnp.concatenate([_bdot(pk[:, :LANES], wk_ref[l][:, :LANES]),
                                  _bdot(pk[:, LANES:], wk_ref[l][:, LANES:])], axis=1)
            tv = jnp.concatenate([_bdot(pv[:, :LANES], wv_ref[l][:, :LANES]),
                                  _bdot(pv[:, LANES:], wv_ref[l][:, LANES:])], axis=1)
            bk = bk + jnp.where(rid == l, tk, 0.0)
            bv = bv + jnp.where(rid == l, tv, 0.0)
        bias_s[0:1, :] = jnp.sum(bk, axis=0, keepdims=True)
        bias_s[1:2, :] = jnp.sum(bv, axis=0, keepdims=True)

    @pl.when(step + 1 < total)
    def _():
        start(step + 1, 1 - slot)

    for j in range(CMP_PAGES):
        for kind in range(2):
            _cmp_copy(pt_ref, cache_ref, buf, sem, step, slot, j, kind, n_steps, page).wait()

    fk = jnp.zeros((nchunk, 2 * LANES), F32)
    fv = jnp.zeros((nchunk, 2 * LANES), F32)
    for l in range(CMP_STRIDE):
        xk = buf[slot, 0, pl.ds(l, nchunk, stride=CMP_STRIDE), :].astype(BF16)
        xv = buf[slot, 1, pl.ds(l, nchunk, stride=CMP_STRIDE), :].astype(BF16)
        fk = fk + jnp.dot(xk, wk_ref[l], preferred_element_type=F32)
        fv = fv + jnp.dot(xv, wv_ref[l], preferred_element_type=F32)
    m0 = pl.multiple_of(si * nchunk, nchunk)
    f_s[pl.ds(m0, nchunk), 0:2 * LANES] = fk
    f_s[pl.ds(m0, nchunk), 2 * LANES:4 * LANES] = fv

    @pl.when(si == n_steps - 1)
    def _():
        ncp = f_s.shape[0]
        last = lax.broadcasted_iota(jnp.int32, (ncp, LANES), 0) == ncp - 1
        kc = f_s[:, 0:LANES] + pltpu.roll(f_s[:, LANES:2 * LANES], ncp - 1, axis=0)
        kc = kc + bias_s[0:1, 0:LANES] + bias_s[0:1, LANES:2 * LANES]
        kc = jnp.where(last, 0.0, kc)
        kc = kc * lax.rsqrt(_segmean(kc * kc, e128_ref) + EPS) * kn0_ref[...]
        hi, lo = _split(kc)
        kch_ref[0] = hi
        kcl_ref[0] = lo
        vc = f_s[:, 2 * LANES:3 * LANES] + pltpu.roll(f_s[:, 3 * LANES:4 * LANES], ncp - 1, axis=0)
        vc = vc + bias_s[1:2, 0:LANES] + bias_s[1:2, LANES:2 * LANES]
        vct_ref[0] = jnp.where(last, 0.0, vc).T.astype(BF16)


def _compress_call(cache, page_table, lw):
    b, n_pages = page_table.shape
    page = cache.shape[1]
    assert n_pages % CMP_PAGES == 0 and page % CMP_STRIDE == 0
    n_steps = n_pages // CMP_PAGES
    ncp = n_pages * page // CMP_STRIDE
    rows = CMP_PAGES * page
    consts = (lw['wk'], lw['wv'], lw['pek'], lw['pev'], lw['kn0'], lw['e128'])
    out3 = lambda s: pl.BlockSpec((1,) + s, lambda i, j, pt: (i, 0, 0))
    grid_spec = pltpu.PrefetchScalarGridSpec(
        num_scalar_prefetch=1, grid=(b, n_steps),
        in_specs=[pl.BlockSpec(memory_space=pl.ANY)]
        + [pl.BlockSpec(c.shape, lambda i, j, pt, nd=c.ndim: (0,) * nd) for c in consts],
        out_specs=[out3((ncp, LANES)), out3((ncp, LANES)), out3((LANES, ncp))],
        scratch_shapes=[pltpu.VMEM((2, 2, rows, LANES), F32), pltpu.SemaphoreType.DMA((2,)),
                        pltpu.VMEM((ncp, 4 * LANES), F32), pltpu.VMEM((SUBLANES, 2 * LANES), F32)])
    return pl.pallas_call(
        functools.partial(_compress_kernel, n_steps=n_steps, page=page, total=b * n_steps),
        grid_spec=grid_spec,
        out_shape=[jax.ShapeDtypeStruct((b, ncp, LANES), BF16), jax.ShapeDtypeStruct((b, ncp, LANES), BF16),
                   jax.ShapeDtypeStruct((b, LANES, ncp), BF16)],
        compiler_params=_params(("arbitrary", "arbitrary")), name="compress",
    )(page_table, cache, *consts)


def _softmax_cols(s, mask):
    sm = jnp.where(mask, s, NEG)
    m = jnp.max(sm, axis=0, keepdims=True)
    p = jnp.where(mask, jnp.exp(sm - m), 0.0)
    return p, m, jnp.sum(p, axis=0, keepdims=True)


def _flash_update(carry, s, mask, vt):
    m, l, acc = carry
    sm = jnp.where(mask, s, NEG)
    m_new = jnp.maximum(m, jnp.max(sm, axis=0, keepdims=True))
    alpha = jnp.exp(m - m_new)
    p = jnp.where(mask, jnp.exp(sm - m_new), 0.0)
    l = alpha * l + jnp.sum(p, axis=0, keepdims=True)
    acc = acc * alpha + jnp.dot(vt, p.astype(BF16), preferred_element_type=F32)
    return m_new, l, acc


def _compressed_branch(kch, kcl, vct, qp, colpos):
    qh, ql = _split(qp)
    s = _nt(kch, qh) + _nt(kch, ql) + _nt(kcl, qh)
    n_idx = lax.broadcasted_iota(jnp.int32, s.shape, 0)
    mask = n_idx * CMP_STRIDE + (2 * CMP_STRIDE - 1) <= colpos
    p, _, l = _softmax_cols(s, mask)
    p = p / jnp.maximum(l, 1e-30)
    return p, jnp.dot(vct, p.astype(BF16), preferred_element_type=F32)


def _block_scores(mbt_ref, imp, colpos):
    ih, il = _split(imp)
    blk = jnp.dot(mbt_ref[...], ih, preferred_element_type=F32) + jnp.dot(mbt_ref[...], il, preferred_element_type=F32)
    j_idx = lax.broadcasted_iota(jnp.int32, blk.shape, 0)
    avail = j_idx * SEL_BLOCK <= colpos
    forced = (j_idx == colpos // SEL_BLOCK) | (j_idx == 0)
    return jnp.where(avail, jnp.where(forced, FORCE, blk), NEG)


def _pick_round(x, j_f):
    m = jnp.max(x, axis=0, keepdims=True)
    first = jnp.min(jnp.where(x == m, j_f, 1e9), axis=0, keepdims=True)
    pick = j_f == first
    return m, first, pick, jnp.where(pick, -jnp.inf, x)


def _head_rows_to_lanes(o_rows, gates, rows_per_head):
    n = rows_per_head
    lo_half = lax.broadcasted_iota(jnp.int32, (n, LANES), 1) < NSA_HD
    outs = []
    for h in range(NSA_HEADS):
        acc = None
        for c, o in enumerate(o_rows):
            term = gates[:, 3 * h + c:3 * h + c + 1] * o[h * n:(h + 1) * n, :]
            acc = term if acc is None else acc + term
        outs.append(acc)
    y01 = jnp.where(lo_half, outs[0], pltpu.roll(outs[1], NSA_HD, axis=1))
    y23 = jnp.where(lo_half, pltpu.roll(outs[2], NSA_HD, axis=1), outs[3])
    return jnp.concatenate([y01, y23], axis=1)


def _nsa_prompt_kernel(q_ref, ng_ref, kch_ref, kcl_ref, vct_ref, mbt_ref, ks_ref, svt_ref, kw_ref, wvt_ref,
                       e16_ref, y_ref, sel_s):
    qb_rows = Q_BLOCK
    s0 = pl.program_id(1) * qb_rows
    q = q_ref[0]
    lo_half = lax.broadcasted_iota(jnp.int32, (qb_rows, LANES), 1) < NSA_HD
    q01, q23 = q[:, :LANES], q[:, LANES:]
    qp = jnp.concatenate([jnp.where(lo_half, q01, 0.0),
                          jnp.where(lo_half, pltpu.roll(q01, NSA_HD, axis=1), 0.0),
                          jnp.where(lo_half, 0.0, pltpu.roll(q23, NSA_HD, axis=1)),
                          jnp.where(lo_half, 0.0, q23)], axis=0)
    ncols = NSA_HEADS * qb_rows
    colpos = s0 + lax.broadcasted_iota(jnp.int32, (1, ncols), 1) % qb_rows
    qb = qp.astype(BF16)

    p, oc_t = _compressed_branch(kch_ref[0], kcl_ref[0], vct_ref[0], qp, colpos)
    imp = jnp.concatenate([p[:, 0:qb_rows] + p[:, qb_rows:2 * qb_rows],
                           p[:, 2 * qb_rows:3 * qb_rows] + p[:, 3 * qb_rows:]], axis=1)
    x = _block_scores(mbt_ref, imp, colpos[:, :2 * qb_rows])
    j_f = lax.broadcasted_iota(jnp.int32, x.shape, 0).astype(F32)
    sel = jnp.zeros(x.shape, F32)
    for _ in range(N_SEL):
        m, _, pick, x = _pick_round(x, j_f)
        sel = jnp.where(pick & (m > 0.5 * NEG), 1.0, sel)
    sel_s[...] = jnp.concatenate([sel[:, :qb_rows], sel[:, :qb_rows], sel[:, qb_rows:], sel[:, qb_rows:]], axis=1)

    kt = SEL_TILE
    nblk = kt // SEL_BLOCK

    def body(c, carry):
        k0 = pl.multiple_of(c * kt, kt)
        s = _nt(ks_ref[0, pl.ds(k0, kt), :], qb)
        member = jnp.dot(e16_ref[...], sel_s[pl.ds(pl.multiple_of(c * nblk, nblk), nblk), :].astype(BF16),
                         preferred_element_type=F32)
        kpos = k0 + lax.broadcasted_iota(jnp.int32, s.shape, 0)
        mask = (member > 0.5) & (kpos <= colpos)
        return _flash_update(carry, s, mask, svt_ref[0, :, pl.ds(k0, kt)])

    init = (jnp.full((1, ncols), NEG, F32), jnp.zeros((1, ncols), F32), jnp.zeros((LANES, ncols), F32))
    _, l, acc = lax.fori_loop(0, (s0 + qb_rows + kt - 1) // kt, body, init)
    os_t = acc / jnp.maximum(l, 1e-30)

    span = WINDOW + qb_rows
    w0 = pl.multiple_of(jnp.maximum(s0 - WINDOW, 0), qb_rows)
    s = _nt(kw_ref[0, pl.ds(w0, span), :], qb)
    dpos = colpos - (w0 + lax.broadcasted_iota(jnp.int32, s.shape, 0))
    pw, _, lw_ = _softmax_cols(s, (dpos >= 0) & (dpos < WINDOW))
    ow_t = jnp.dot(wvt_ref[0, :, pl.ds(w0, span)], pw.astype(BF16), preferred_element_type=F32) / jnp.maximum(lw_, 1e-30)

    y_ref[0] = _head_rows_to_lanes([oc_t.T, os_t.T, ow_t.T], ng_ref[0], qb_rows)


def _nsa_prompt_call(q, ng, kch, kcl, vct, ksb, svt, kwb, wvt, lw):
    b, t, _ = q.shape
    assert t % SEL_TILE == 0 and t >= WINDOW + Q_BLOCK
    ncp = kch.shape[1]
    nbp = t // SEL_BLOCK
    full = lambda s: pl.BlockSpec((1,) + s, lambda i, j: (i, 0, 0))
    blk = lambda w: pl.BlockSpec((1, Q_BLOCK, w), lambda i, j: (i, j, 0))
    return pl.pallas_call(
        _nsa_prompt_kernel, grid=(b, t // Q_BLOCK),
        in_specs=[blk(2 * LANES), blk(LANES), full((ncp, LANES)), full((ncp, LANES)), full((LANES, ncp)),
                  _const_spec(lw['mbt_p'].shape), full((t, LANES)), full((LANES, t)), full((t, LANES)),
                  full((LANES, t)), _const_spec(lw['e16'].shape)],
        out_specs=blk(2 * LANES), out_shape=jax.ShapeDtypeStruct((b, t, 2 * LANES), F32),
        scratch_shapes=[pltpu.VMEM((nbp, NSA_HEADS * Q_BLOCK), F32)],
        compiler_params=_params(("arbitrary", "arbitrary")), name="nsa_prompt",
    )(q, ng, kch, kcl, vct, lw['mbt_p'], ksb, svt, kwb, wvt, lw['e16'])


_SCOL = SUBLANES


def _nsa_sample_a_kernel(qp_ref, kch_ref, kcl_ref, vct_ref, mbt_ref, cwin_ref, wnew_ref,
                         oc_ref, ow_ref, idx_ref, ok_ref, *, past_len, n_dec):
    qp = qp_ref[0]
    qb = qp.astype(BF16)
    col = lax.broadcasted_iota(jnp.int32, (1, LANES), 1)
    tcol = col % _SCOL
    colpos = past_len + tcol
    p, oc_t = _compressed_branch(kch_ref[0], kcl_ref[0], vct_ref[0], qp, colpos)
    oc_ref[0] = oc_t
    first_head = (col % (2 * _SCOL)) < _SCOL
    imp = p + jnp.where(first_head, pltpu.roll(p, LANES - _SCOL, axis=1), pltpu.roll(p, _SCOL, axis=1))
    x = _block_scores(mbt_ref, imp, colpos)
    j_f = lax.broadcasted_iota(jnp.int32, x.shape, 0).astype(F32)
    for rd in range(N_SEL - 1):
        m, first, _, x = _pick_round(x, j_f)
        idx_ref[0, rd:rd + 1, :] = first.astype(jnp.int32)
        ok_ref[0, rd:rd + 1, :] = jnp.where(m > 0.5 * NEG, 1.0, 0.0)
    idx_ref[0, N_SEL - 1:N_SEL, :] = jnp.zeros((1, LANES), jnp.int32)
    ok_ref[0, N_SEL - 1:N_SEL, :] = jnp.ones((1, LANES), F32)

    kp, kn = cwin_ref[0], wnew_ref[0]
    wbuf = kp.shape[0]
    s1 = _nt(kp[:, :LANES].astype(BF16), qb)
    kpos1 = past_len - wbuf + lax.broadcasted_iota(jnp.int32, s1.shape, 0)
    d1 = colpos - kpos1
    mask1 = (d1 >= 0) & (d1 < WINDOW) & (kpos1 >= 0)
    s2 = _nt(kn[:, :LANES].astype(BF16), qb)
    kidx = lax.broadcasted_iota(jnp.int32, s2.shape, 0)
    mask2 = (kidx <= tcol) & (kidx < n_dec)
    carry = (jnp.full((1, LANES), NEG, F32), jnp.zeros((1, LANES), F32), jnp.zeros((LANES, LANES), F32))
    carry = _flash_update(carry, s1, mask1, kp[:, LANES:].T.astype(BF16))
    _, l, acc = _flash_update(carry, s2, mask2, kn[:, LANES:].T.astype(BF16))
    ow_ref[0] = acc / jnp.maximum(l, 1e-30)


def _nsa_sample_a_call(qp, kch, kcl, vct, cwin, wnew, lw, past_len, n_dec):
    b = qp.shape[0]
    ncp = kch.shape[1]
    full = lambda a: pl.BlockSpec((1,) + a.shape[1:], lambda i: (i, 0, 0))
    sq = pl.BlockSpec((1, LANES, LANES), lambda i: (i, 0, 0))
    rnd = pl.BlockSpec((1, N_SEL, LANES), lambda i: (i, 0, 0))
    return pl.pallas_call(
        functools.partial(_nsa_sample_a_kernel, past_len=past_len, n_dec=n_dec), grid=(b,),
        in_specs=[sq, full(kch), full(kcl), full(vct), _const_spec(lw['mbt_s'].shape), full(cwin), full(wnew)],
        out_specs=[sq, sq, rnd, rnd],
        out_shape=[jax.ShapeDtypeStruct((b, LANES, LANES), F32)] * 2
        + [jax.ShapeDtypeStruct((b, N_SEL, LANES), jnp.int32), jax.ShapeDtypeStruct((b, N_SEL, LANES), F32)],
        compiler_params=_params(("parallel",)), name="nsa_sample_a",
    )(qp, kch, kcl, vct, lw['mbt_s'], cwin, wnew)


def _sel_copy(pt_ref, sel_ref, cache_ref, buf, sem, bi, c, rd, page):
    j = sel_ref[bi, c * (N_SEL - 1) + rd]
    per_page = page // SEL_BLOCK
    pg = pt_ref[bi, j // per_page]
    return pltpu.make_async_copy(
        cache_ref.at[pg, pl.ds((j % per_page) * SEL_BLOCK, SEL_BLOCK), pl.ds(2 * LANES, 2 * LANES)],
        buf.at[c, pl.ds(rd * SEL_BLOCK, SEL_BLOCK), :], sem)


def _nsa_sample_b_kernel(pt_ref, sel_ref, cache_ref, qp_ref, new_ref, ok_ref, oc_ref, ow_ref, ng_ref, e64_ref,
                         y_ref, buf, sem, *, n_dec, page):
    bi = pl.program_id(0)
    ncomb = NSA_KV * n_dec
    past_rows = (N_SEL - 1) * SEL_BLOCK
    for c in range(ncomb):
        for rd in range(N_SEL - 1):
            _sel_copy(pt_ref, sel_ref, cache_ref, buf, sem, bi, c, rd, page).start()
    for c in range(ncomb):
        buf[c, past_rows:past_rows + SEL_BLOCK, :] = new_ref[0, 0:SEL_BLOCK, :]
    for c in range(ncomb):
        for rd in range(N_SEL - 1):
            _sel_copy(pt_ref, sel_ref, cache_ref, buf, sem, bi, c, rd, page).wait()

    qb = qp_ref[0].astype(BF16)
    col = lax.broadcasted_iota(jnp.int32, (1, LANES), 1)
    carry = (jnp.full((1, LANES), NEG, F32), jnp.zeros((1, LANES), F32), jnp.zeros((LANES, LANES), F32))
    nrows = N_SEL * SEL_BLOCK
    row = lax.broadcasted_iota(jnp.int32, (nrows, LANES), 0)
    for c in range(ncomb):
        g, t = divmod(c, n_dec)
        mine = (col // (NSA_HEADS // NSA_KV * _SCOL) == g) & (col % _SCOL == t)
        okc = jnp.where(mine, ok_ref[0], 0.0)
        member = jnp.dot(e64_ref[...], okc.astype(BF16), preferred_element_type=F32)
        kv = buf[c]
        s = _nt(kv[:, :LANES].astype(BF16), qb)
        mask = (member > 0.5) & ((row < past_rows) | (row - past_rows <= t))
        carry = _flash_update(carry, s, mask, kv[:, LANES:].T.astype(BF16))
    _, l, acc = carry
    os_t = acc / jnp.maximum(l, 1e-30)
    y_ref[0] = _head_rows_to_lanes([oc_ref[0].T, os_t.T, ow_ref[0].T], ng_ref[0], _SCOL)


def _nsa_sample_b_call(page_table, sel, cache, qp, newkv, ok, oc, ow, ng, lw, n_dec):
    b = qp.shape[0]
    page = cache.shape[1]
    ncomb = NSA_KV * n_dec
    sq = pl.BlockSpec((1, LANES, LANES), lambda i, pt, sl: (i, 0, 0))
    grid_spec = pltpu.PrefetchScalarGridSpec(
        num_scalar_prefetch=2, grid=(b,),
        in_specs=[pl.BlockSpec(memory_space=pl.ANY), sq,
                  pl.BlockSpec((1, LANES, 2 * LANES), lambda i, pt, sl: (i, 0, 0)),
                  pl.BlockSpec((1, N_SEL, LANES), lambda i, pt, sl: (i, 0, 0)), sq, sq,
                  pl.BlockSpec((1, _SCOL, LANES), lambda i, pt, sl: (i, 0, 0)),
                  pl.BlockSpec(lw['e64'].shape, lambda i, pt, sl: (0, 0))],
        out_specs=pl.BlockSpec((1, _SCOL, 2 * LANES), lambda i, pt, sl: (i, 0, 0)),
        scratch_shapes=[pltpu.VMEM((ncomb, N_SEL * SEL_BLOCK, 2 * LANES), F32), pltpu.SemaphoreType.DMA(())])
    return pl.pallas_call(
        functools.partial(_nsa_sample_b_kernel, n_dec=n_dec, page=page), grid_spec=grid_spec,
        out_shape=jax.ShapeDtypeStruct((b, _SCOL, 2 * LANES), F32),
        compiler_params=_params(("arbitrary",)), name="nsa_sample_b",
    )(page_table, sel, cache, qp, newkv, ok, oc, ow, ng, lw['e64'])


def _merge_kernel(x_ref, g1_ref, wg_ref, ya_ref, yb_ref, yc_ref, yd_ref, wout_ref, wo_ref, o_ref):
    x = x_ref[...]
    h = _rms(x, g1_ref[...]).astype(BF16)
    mix = None
    for i, y_ref in enumerate((ya_ref, yb_ref, yc_ref, yd_ref)):
        gate = jax.nn.sigmoid(jnp.dot(h, wg_ref[:, i * D_MODEL:(i + 1) * D_MODEL], preferred_element_type=F32))
        term = gate * _bdot(y_ref[...], wout_ref[i])
        mix = term if mix is None else mix + term
    o_ref[...] = x + _bdot(mix, wo_ref[...])


def _merge_call(x, ys, lw):
    n = x.shape[0]
    tm = min(512, n)
    row = lambda w: pl.BlockSpec((tm, w), lambda i: (i, 0))
    return pl.pallas_call(
        _merge_kernel, grid=(n // tm,),
        in_specs=[row(D_MODEL), _const_spec(lw['g1'].shape), _const_spec(lw['wgate'].shape)] + [row(2 * LANES)] * 4
        + [_const_spec(lw['wout'].shape), _const_spec(lw['wo'].shape)],
        out_specs=row(D_MODEL), out_shape=jax.ShapeDtypeStruct((n, D_MODEL), F32),
        compiler_params=_params(("parallel",)), name="merge",
    )(x, lw['g1'], lw['wgate'], *ys, lw['wout'], lw['wo'])


_FFN_CHUNK = 256


def _ffn_kernel(x_ref, g2_ref, wg_ref, wu_ref, wd_ref, o_ref):
    x = x_ref[...]
    h = _rms(x, g2_ref[...]).astype(BF16)
    acc = x
    for c in range(0, D_FF, _FFN_CHUNK):
        g = jnp.dot(h, wg_ref[:, c:c + _FFN_CHUNK], preferred_element_type=F32)
        u = jnp.dot(h, wu_ref[:, c:c + _FFN_CHUNK], preferred_element_type=F32)
        acc = acc + _bdot(g * jax.nn.sigmoid(g) * u, wd_ref[c:c + _FFN_CHUNK, :])
    o_ref[...] = acc


def _ffn_call(x, lw):
    n = x.shape[0]
    tm = min(512, n)
    row = pl.BlockSpec((tm, D_MODEL), lambda i: (i, 0))
    return pl.pallas_call(
        _ffn_kernel, grid=(n // tm,),
        in_specs=[row, _const_spec(lw['g2'].shape), _const_spec(lw['fg'].shape), _const_spec(lw['fu'].shape),
                  _const_spec(lw['fd'].shape)],
        out_specs=row, out_shape=jax.ShapeDtypeStruct((n, D_MODEL), F32),
        compiler_params=_params(("parallel",)), name="ffn",
    )(x, lw['g2'], lw['fg'], lw['fu'], lw['fd'])


def _block_diag(blocks):
    g, a, b = blocks.shape[-3:]
    eye = jnp.eye(g, dtype=blocks.dtype)
    out = jnp.einsum('...gab,gh->...gahb', blocks, eye)
    return out.reshape(blocks.shape[:-3] + (g * a, g * b))


def _seg_mean_matrix(width):
    seg = jnp.arange(width) // NSA_HD
    return jnp.where(seg[:, None] == seg[None, :], 1.0 / NSA_HD, 0.0).astype(BF16)


def _band_matrix(nbp, ncp):
    ratio = SEL_BLOCK // CMP_STRIDE
    j = jnp.arange(nbp)[:, None]
    n = jnp.arange(ncp)[None, :]
    return ((n >= ratio * j - 1) & (n <= ratio * j + ratio - 1)).astype(BF16)


def _expand_matrix(nrows, nblk):
    return (jnp.arange(nrows)[:, None] // SEL_BLOCK == jnp.arange(nblk)[None, :]).astype(BF16)


def _prep_layer(w, l, t_prompt, past_len):
    row = lambda a: a.reshape(1, -1).astype(F32)
    w_in = w['w_in'][l]
    pad_cols = lambda a, n: jnp.pad(a, ((0, 0), (0, n - a.shape[1])))
    wp = jnp.concatenate([w_in[:, 0:1280], pad_cols(w_in[:, 1280:1296], LANES), w_in[:, 1296:2576],
                          pad_cols(w_in[:, 2576:2588], LANES)], axis=1).astype(BF16)
    phi, pe = w['nsa_phi'][l], w['nsa_pe'][l]
    half = CMP_STRIDE

    def cmp_w(p):
        bd = lambda a: _block_diag(jnp.broadcast_to(a[:, None], (half, NSA_KV) + a.shape[1:]))
        return jnp.concatenate([bd(p[:half]), bd(p[half:])], axis=-1).astype(BF16)

    def cmp_pe(p):
        return jnp.concatenate([jnp.tile(p[:half], (1, NSA_KV)), jnp.tile(p[half:], (1, NSA_KV))], axis=1).astype(F32)

    kw = GLA_HEADS * GLA_DK
    hk = jnp.arange(kw) // GLA_DK
    hv = jnp.arange(GLA_HEADS * GLA_DV) // GLA_DV
    same_head = hk[:, None] == hv[None, :]
    return {
        'g1': row(w['norm1'][l]), 'g2': row(w['norm2'][l]), 'wp': wp, 'wgate': w_in[:, 2588:].astype(BF16),
        'wa2': jnp.pad(w['gla_wa2'][l], ((0, LANES - GLA_LOWRANK), (0, 0))).astype(BF16), 'ba': row(w['gla_ba'][l]),
        'qn': row(jnp.tile(w['nsa_qn'][l], NSA_HEADS)) * (NSA_HD ** -0.5),
        'kn0': row(jnp.tile(w['nsa_kn'][l][0], NSA_KV)), 'kn1': row(jnp.tile(w['nsa_kn'][l][1], NSA_KV)),
        'kn2': row(jnp.tile(w['nsa_kn'][l][2], NSA_KV)),
        'gb': row(jnp.pad(w['nsa_gb'][l], (0, LANES - 3 * NSA_HEADS))),
        'e256': _seg_mean_matrix(2 * LANES), 'e128': _seg_mean_matrix(LANES),
        'cw': jnp.pad(w['conv_w'][l], ((0, 1), (0, 0))).astype(F32), 'cb': row(w['conv_b'][l]),
        'clg': row(w['conv_ln_g'][l]), 'clb': row(w['conv_ln_b'][l]),
        'pw': _block_diag(w['pool_w'][l]).astype(BF16), 'psc': row(w['pool_scale'][l]),
        'on': row(jnp.tile(w['gla_onorm'][l], GLA_HEADS)),
        'eh': same_head.astype(BF16), 'mbd': same_head.T.astype(F32),
        'wk': cmp_w(phi[0]), 'wv': cmp_w(phi[1]), 'pek': cmp_pe(pe[0]), 'pev': cmp_pe(pe[1]),
        'mbt_p': _band_matrix(t_prompt // SEL_BLOCK, t_prompt // CMP_STRIDE),
        'mbt_s': _band_matrix(past_len // SEL_BLOCK, past_len // CMP_STRIDE),
        'e16': _expand_matrix(SEL_TILE, SEL_TILE // SEL_BLOCK), 'e64': _expand_matrix(N_SEL * SEL_BLOCK, N_SEL),
        'wout': jnp.stack([w['w_out_conv'][l], w['w_out_gla'][l], w['w_out_pool'][l], w['w_out_nsa'][l]]).astype(BF16),
        'wo': w['w_o'][l].astype(BF16),
        'fg': w['ffn_gate'][l].astype(BF16), 'fu': w['ffn_up'][l].astype(BF16), 'fd': w['ffn_down'][l].astype(BF16),
    }


def _prompt_layer(x, lw):
    b, t, _ = x.shape
    xf = x.reshape(b * t, D_MODEL)
    a, qk, v, og, la, up, q, rows, win, ng, ksb, svt, kwb, wvt = _proj_call(xf, lw, True)
    r3 = lambda z: z.reshape(b, t, z.shape[-1])
    ya, u = _conv_call(r3(a), jnp.zeros((b, CONV_WIDTH - 1, CONV_CH), F32), lw)
    yc = _pool_call(r3(up), jnp.zeros((b, POOL_STATE, POOL_CH), F32), 0, lw)
    yb, gla_state = _gla_call(r3(qk), r3(v), r3(la), r3(og), jnp.zeros((b, GLA_HEADS, GLA_DK, GLA_DV), F32), lw)
    page = 2 * SEL_BLOCK
    cache = rows.reshape(b * t // page, page, rows.shape[-1])
    table = jnp.arange(b * t // page, dtype=jnp.int32).reshape(b, t // page)
    kch, kcl, vct = _compress_call(cache, table, lw)
    tcols = lambda z: z.reshape(LANES, b, t).swapaxes(0, 1)
    yd = _nsa_prompt_call(r3(q), r3(ng), kch, kcl, vct, r3(ksb), tcols(svt), r3(kwb), tcols(wvt), lw)
    x1 = _merge_call(xf, [ya.reshape(b * t, -1), yb.reshape(b * t, -1), yc.reshape(b * t, -1), yd.reshape(b * t, -1)], lw)
    x2 = _ffn_call(x1, lw).reshape(b, t, D_MODEL)
    keep = min(WINDOW, t)
    return (x2, r3(rows).reshape(b, t, 4, NSA_KV, NSA_HD), r3(win)[:, -keep:].reshape(b, keep, 2, NSA_KV, NSA_HD),
            u[:, -(CONV_WIDTH - 1):], r3(up)[:, -POOL_STATE:], gla_state)


def _sample_layer(x, lw, cache, cwin, conv_prev, pool_prev, gla_prev, page_table):
    b, t, _ = x.shape
    page = cache.shape[1]
    past_len = page_table.shape[1] * page
    assert past_len % SEL_BLOCK == 0 and t <= _SCOL and page % SEL_BLOCK == 0
    xf = x.reshape(b * t, D_MODEL)
    a, qk, v, og, la, up, q, rows, win, ng = _proj_call(xf, lw, False)
    r3 = lambda z: z.reshape(b, t, z.shape[-1])
    ya, u = _conv_call(r3(a), conv_prev, lw)
    yc = _pool_call(r3(up), pool_prev, past_len, lw)
    yb, gla_state = _gla_call(r3(qk), r3(v), r3(la), r3(og), gla_prev, lw)
    cache2 = cache
    kch, kcl, vct = _compress_call(cache2, page_table, lw)
    q4 = jnp.pad(r3(q).reshape(b, t, NSA_HEADS, NSA_HD), ((0, 0), (0, _SCOL - t), (0, 0), (0, 0))).swapaxes(1, 2)
    grp = jnp.arange(NSA_HEADS) // (NSA_HEADS // NSA_KV)
    onehot = (grp[:, None] == jnp.arange(NSA_KV)[None, :]).astype(F32)
    qp = jnp.einsum('bhtd,hg->bhtgd', q4, onehot).reshape(b, NSA_HEADS * _SCOL, LANES)
    qp = jnp.pad(qp, ((0, 0), (0, LANES - NSA_HEADS * _SCOL), (0, 0)))
    pad_rows = lambda z, n: jnp.pad(z, ((0, 0), (0, n - z.shape[1]), (0, 0)))
    cwin2 = cwin.reshape(b, cwin.shape[1], -1)
    oc, ow, idx, ok = _nsa_sample_a_call(qp, kch, kcl, vct, cwin2, pad_rows(r3(win), LANES), lw, past_len, t)
    cols = (jnp.arange(NSA_KV)[:, None] * (NSA_HEADS // NSA_KV) * _SCOL + jnp.arange(t)[None, :]).reshape(-1)
    sel = idx[:, :N_SEL - 1, :][:, :, cols].swapaxes(1, 2).reshape(b, -1)
    yd = _nsa_sample_b_call(page_table, sel, cache2, qp, pad_rows(r3(rows)[:, :, 2 * LANES:], LANES), ok, oc, ow,
                            pad_rows(r3(ng), _SCOL), lw, t)[:, :t]
    x1 = _merge_call(xf, [ya.reshape(b * t, -1), yb.reshape(b * t, -1), yc.reshape(b * t, -1), yd.reshape(b * t, -1)], lw)
    x2 = _ffn_call(x1, lw).reshape(b, t, D_MODEL)
    win_all = jnp.concatenate([cwin2, r3(win)], axis=1)
    keep = min(WINDOW, win_all.shape[1])
    return (x2, r3(rows).reshape(b, t, 4, NSA_KV, NSA_HD), win_all[:, -keep:].reshape(b, keep, 2, NSA_KV, NSA_HD),
            jnp.concatenate([conv_prev, u], axis=1)[:, -(CONV_WIDTH - 1):],
            jnp.concatenate([pool_prev, r3(up)], axis=1)[:, -POOL_STATE:], gla_state)


def kernel(x_prompt, x_sample, cache_nsa_kv, cache_win_kv, state_conv, state_pool, state_gla, page_table, norm1, w_in, conv_w, conv_b, conv_ln_g, conv_ln_b, w_out_conv, gla_wa2, gla_ba, gla_onorm, w_out_gla, pool_w, pool_scale, w_out_pool, nsa_qn, nsa_kn, nsa_pe, nsa_phi, nsa_gb, w_out_nsa, w_o, norm2, ffn_gate, ffn_up, ffn_down):
    w = dict(norm1=norm1, w_in=w_in, conv_w=conv_w, conv_b=conv_b, conv_ln_g=conv_ln_g, conv_ln_b=conv_ln_b,
             w_out_conv=w_out_conv, gla_wa2=gla_wa2, gla_ba=gla_ba, gla_onorm=gla_onorm, w_out_gla=w_out_gla,
             pool_w=pool_w, pool_scale=pool_scale, w_out_pool=w_out_pool, nsa_qn=nsa_qn, nsa_kn=nsa_kn,
             nsa_pe=nsa_pe, nsa_phi=nsa_phi, nsa_gb=nsa_gb, w_out_nsa=w_out_nsa, w_o=w_o, norm2=norm2,
             ffn_gate=ffn_gate, ffn_up=ffn_up, ffn_down=ffn_down)
    depth, n_pool, page = cache_nsa_kv.shape[:3]
    past_len = page_table.shape[1] * page
    cache = cache_nsa_kv.reshape(depth * n_pool, page, -1)
    xp, xs = x_prompt, x_sample
    outs_p, outs_s = [], []
    for l in range(depth):
        lw = _prep_layer(w, l, xp.shape[1], past_len)
        xp, *st_p = _prompt_layer(xp, lw)
        xs, *st_s = _sample_layer(xs, lw, cache, cache_win_kv[l], state_conv[l], state_pool[l],
                                  state_gla[l], page_table + l * n_pool)
        outs_p.append(st_p)
        outs_s.append(st_s)
    stack = lambda outs, i: jnp.stack([o[i] for o in outs])
    return (xp, xs) + tuple(stack(outs_p, i) for i in range(5)) + tuple(stack(outs_s, i) for i in range(5))
```

```python
import functools

import jax
import jax.numpy as jnp
from jax import lax
from jax.experimental import pallas as pl
from jax.experimental.pallas import tpu as pltpu

F32, BF16 = jnp.float32, jnp.bfloat16

D_MODEL = 1024
CONV_CH = 256
CONV_WIDTH = 31
GLA_HEADS, GLA_DK, GLA_DV = 4, 32, 64
GLA_LOWRANK = 16
GLA_TAU = 16.0
POOL_CH = 256
POOL_WINDOWS = (2, 4, 8, 16)
POOL_STATE = 15
NSA_HEADS, NSA_HD, NSA_KV = 4, 64, 2
CMP_STRIDE = 16
SEL_BLOCK = 64
N_SEL = 16
WINDOW = 512
Q_BLOCK = 128
D_FF = 2816
EPS = 1e-6
NEG = -1e30
FORCE = 1e4

LANES = 128
SUBLANES = 8
VMEM_LIMIT = 56 * 1024 * 1024

GLA_SUB = 16
SEL_TILE = 1024
CMP_PAGES = 16

_P_A, _P_QK, _P_V, _P_O, _P_LR, _P_UP, _P_NQ, _P_KV, _P_NG, _P_END = (
    0, 512, 768, 1024, 1280, 1408, 1664, 1920, 2688, 2816)


def _bdot(a, b):
    return jnp.dot(a.astype(BF16), b.astype(BF16), preferred_element_type=F32)


def _nt(a, b):
    return lax.dot_general(a, b, (((1,), (1,)), ((), ())), preferred_element_type=F32)


def _tn(a, b):
    return lax.dot_general(a, b, (((0,), (0,)), ((), ())), preferred_element_type=F32)


def _split(x):
    hi = x.astype(BF16)
    return hi, (x - hi.astype(F32)).astype(BF16)


def _rms(x, g):
    return x * lax.rsqrt(jnp.mean(x * x, axis=-1, keepdims=True) + EPS) * g


def _segmean(x2, e_ref):
    hi, lo = _split(x2)
    e = e_ref[...]
    return jnp.dot(hi, e, preferred_element_type=F32) + jnp.dot(lo, e, preferred_element_type=F32)


def _const_spec(shape):
    nd = len(shape)
    return pl.BlockSpec(shape, lambda *_: (0,) * nd)


def _params(sem):
    return pltpu.CompilerParams(dimension_semantics=sem, vmem_limit_bytes=VMEM_LIMIT)


def _proj_kernel(x_ref, g1_ref, wp_ref, wa2_ref, ba_ref, qn_ref, kn1_ref, kn2_ref, gb_ref, e256_ref, e128_ref,
                 a_ref, qk_ref, v_ref, o_ref, la_ref, up_ref, q_ref, rows_ref, win_ref, ng_ref, *t_refs):
    h = _rms(x_ref[...], g1_ref[...]).astype(BF16)

    def seg(a, b):
        return _nt(h, wp_ref[a:b, :])

    a_ref[...] = seg(_P_A, _P_QK)
    qk_ref[...] = seg(_P_QK, _P_V)
    v_ref[...] = seg(_P_V, _P_O)
    o_ref[...] = seg(_P_O, _P_LR)
    z = _bdot(seg(_P_LR, _P_UP), wa2_ref[...]) + ba_ref[...]
    la_ref[...] = (jnp.minimum(z, 0.0) - jnp.log(1.0 + jnp.exp(-jnp.abs(z)))) * (1.0 / GLA_TAU)
    up_ref[...] = seg(_P_UP, _P_NQ)
    qr = seg(_P_NQ, _P_KV)
    q_ref[...] = qr * lax.rsqrt(_segmean(qr * qr, e256_ref) + EPS) * qn_ref[...]
    kv = seg(_P_KV, _P_NG)
    k_sel = kv[:, 256:384]
    k_sel = k_sel * lax.rsqrt(_segmean(k_sel * k_sel, e128_ref) + EPS) * kn1_ref[...]
    k_win = kv[:, 512:640]
    k_win = k_win * lax.rsqrt(_segmean(k_win * k_win, e128_ref) + EPS) * kn2_ref[...]
    v_sel, v_win = kv[:, 384:512], kv[:, 640:768]
    rows_ref[:, 0:256] = kv[:, 0:256]
    rows_ref[:, 256:384] = k_sel
    rows_ref[:, 384:512] = v_sel
    win_ref[:, 0:128] = k_win
    win_ref[:, 128:256] = v_win
    ng_ref[...] = jax.nn.sigmoid(seg(_P_NG, _P_END) + gb_ref[...])
    if t_refs:
        ksb_ref, svt_ref, kwb_ref, wvt_ref = t_refs
        ksb_ref[...] = k_sel.astype(BF16)
        svt_ref[...] = v_sel.T.astype(BF16)
        kwb_ref[...] = k_win.astype(BF16)
        wvt_ref[...] = v_win.T.astype(BF16)


def _proj_call(x, lw, with_t):
    n = x.shape[0]
    tm = min(512, n)
    row = lambda w: pl.BlockSpec((tm, w), lambda i: (i, 0))
    consts = (lw['g1'], lw['wp'], lw['wa2'], lw['ba'], lw['qn'], lw['kn1'], lw['kn2'], lw['gb'], lw['e256'], lw['e128'])
    widths = (512, 256, 256, 256, 128, 256, 256, 512, 256, 128)
    out_shape = [jax.ShapeDtypeStruct((n, w), F32) for w in widths]
    out_specs = [row(w) for w in widths]
    if with_t:
        col = pl.BlockSpec((LANES, tm), lambda i: (0, i))
        out_shape += [jax.ShapeDtypeStruct((n, LANES), BF16), jax.ShapeDtypeStruct((LANES, n), BF16)] * 2
        out_specs += [row(LANES), col] * 2
    return pl.pallas_call(
        _proj_kernel, grid=(n // tm,),
        in_specs=[row(D_MODEL)] + [_const_spec(c.shape) for c in consts],
        out_specs=out_specs, out_shape=out_shape,
        compiler_params=_params(("parallel",)), name="proj",
    )(x, *consts)


_CONV_PAD = 32


def _conv_kernel(a_ref, prev_ref, w_ref, b_ref, lg_ref, lb_ref, y_ref, u_ref, ext_ref, *, tm):
    @pl.when(pl.program_id(1) == 0)
    def _():
        ext_ref[0:_CONV_PAD, :] = prev_ref[0]

    a = a_ref[0]
    u = a[:, :CONV_CH] * jax.nn.sigmoid(a[:, CONV_CH:])
    ext_ref[_CONV_PAD:_CONV_PAD + tm, :] = u
    acc = jnp.zeros((tm, CONV_CH), F32) + b_ref[...]
    first = _CONV_PAD - (CONV_WIDTH - 1)
    for j in range(CONV_WIDTH):
        acc = acc + ext_ref[first + j:first + j + tm, :] * w_ref[j:j + 1, :]
    mu = jnp.mean(acc, axis=-1, keepdims=True)
    cen = acc - mu
    var = jnp.mean(cen * cen, axis=-1, keepdims=True)
    yn = cen * lax.rsqrt(var + EPS) * lg_ref[...] + lb_ref[...]
    y_ref[0] = yn * jax.nn.sigmoid(yn)
    u_ref[0] = u
    ext_ref[0:_CONV_PAD, :] = ext_ref[tm:tm + _CONV_PAD, :]


def _conv_call(a, prev, lw):
    b, t, _ = a.shape
    tm = min(512, t)
    prev = jnp.pad(prev, ((0, 0), (_CONV_PAD - (CONV_WIDTH - 1), 0), (0, 0)))
    blk = lambda w: pl.BlockSpec((1, tm, w), lambda i, j: (i, j, 0))
    consts = (lw['cw'], lw['cb'], lw['clg'], lw['clb'])
    return pl.pallas_call(
        functools.partial(_conv_kernel, tm=tm), grid=(b, t // tm),
        in_specs=[blk(2 * CONV_CH), pl.BlockSpec((1, _CONV_PAD, CONV_CH), lambda i, j: (i, 0, 0))]
        + [_const_spec(c.shape) for c in consts],
        out_specs=[blk(CONV_CH), blk(CONV_CH)],
        out_shape=[jax.ShapeDtypeStruct((b, t, CONV_CH), F32)] * 2,
        scratch_shapes=[pltpu.VMEM((_CONV_PAD + tm, CONV_CH), F32)],
        compiler_params=_params(("arbitrary", "arbitrary")), name="conv",
    )(a, prev, *consts)


_POOL_PAD = 16


def _pool_kernel(u_ref, prev_ref, w_ref, sc_ref, y_ref, ext_ref, *, tm, pos0):
    t = pl.program_id(1)

    @pl.when(t == 0)
    def _():
        ext_ref[0:_POOL_PAD, :] = prev_ref[0]

    u = u_ref[0]
    ext_ref[_POOL_PAD:_POOL_PAD + tm, :] = u

    def back(d):
        return ext_ref[_POOL_PAD - d:_POOL_PAD - d + tm, :]

    sums, acc, d = [], u, 1
    for w in POOL_WINDOWS:
        while d < w:
            acc = acc + back(d)
            d += 1
        sums.append(acc)
    grp = lax.broadcasted_iota(jnp.int32, (tm, POOL_CH), 1) // (POOL_CH // len(POOL_WINDOWS))
    pos = pos0 + t * tm + lax.broadcasted_iota(jnp.int32, (tm, POOL_CH), 0)
    win, width = sums[-1], jnp.full((tm, POOL_CH), POOL_WINDOWS[-1], jnp.int32)
    for gi in range(len(POOL_WINDOWS) - 2, -1, -1):
        win = jnp.where(grp == gi, sums[gi], win)
        width = jnp.where(grp == gi, POOL_WINDOWS[gi], width)
    cnt = jnp.minimum(pos + 1, width).astype(F32)
    pooled = win / cnt - u
    y_ref[0] = _bdot(pooled, w_ref[...]) * sc_ref[...]
    ext_ref[0:_POOL_PAD, :] = ext_ref[tm:tm + _POOL_PAD, :]


def _pool_call(u, prev, pos0, lw):
    b, t, _ = u.shape
    tm = min(512, t)
    prev = jnp.pad(prev, ((0, 0), (_POOL_PAD - POOL_STATE, 0), (0, 0)))
    blk = pl.BlockSpec((1, tm, POOL_CH), lambda i, j: (i, j, 0))
    return pl.pallas_call(
        functools.partial(_pool_kernel, tm=tm, pos0=pos0), grid=(b, t // tm),
        in_specs=[blk, pl.BlockSpec((1, _POOL_PAD, POOL_CH), lambda i, j: (i, 0, 0)),
                  _const_spec(lw['pw'].shape), _const_spec(lw['psc'].shape)],
        out_specs=blk, out_shape=jax.ShapeDtypeStruct((b, t, POOL_CH), F32),
        scratch_shapes=[pltpu.VMEM((_POOL_PAD + tm, POOL_CH), F32)],
        compiler_params=_params(("arbitrary", "arbitrary")), name="pool",
    )(u, prev, lw['pw'], lw['psc'])


def _gla_kernel(qk_ref, v_ref, la_ref, og_ref, s0_ref, on_ref, eh_ref, mbd_ref, e256_ref,
                y_ref, sfin_ref, st_ref, kext, bext, vext, qs_s, ks_s, g_s, o_s, *, tm):
    R = GLA_SUB
    t = pl.program_id(1)
    kw = GLA_HEADS * GLA_DK

    @pl.when(t == 0)
    def _():
        st_ref[...] = s0_ref[0]
        kext[0:R, :] = jnp.zeros((R, kw), F32)
        bext[0:R, :] = jnp.zeros((R, kw), F32)
        vext[0:R, :] = jnp.zeros((R, GLA_HEADS * GLA_DV), F32)

    qk = qk_ref[0]
    q = qk[:, :kw] * (GLA_DK ** -0.5)
    k = qk[:, kw:]
    v = v_ref[0]
    la = la_ref[0]
    r = lax.broadcasted_iota(jnp.int32, (tm, kw), 0) % R
    b = la
    c = la
    for s in (1, 2, 4, 8):
        b = b + jnp.where(r >= s, pltpu.roll(b, s, axis=0), 0.0)
        c = c + jnp.where(r < R - s, pltpu.roll(c, tm - s, axis=0), 0.0)
    kext[R:R + tm, :] = k
    bext[R:R + tm, :] = b
    vext[R:R + tm, :] = v
    o = jnp.zeros((tm, GLA_HEADS * GLA_DV), F32)
    for d in range(R):
        kd = kext[R - d:R - d + tm, :]
        bd = bext[R - d:R - d + tm, :]
        vd = vext[R - d:R - d + tm, :]
        e = jnp.exp(jnp.where(r >= d, b - bd, NEG))
        a = jnp.dot((q * kd * e).astype(BF16), eh_ref[...], preferred_element_type=F32)
        o = o + a * vd
    o_s[...] = o
    qs_s[...] = q * jnp.exp(b)
    ks_s[...] = k * jnp.exp(c - la)
    g_s[...] = jnp.exp(b + c - la)

    def body(i, carry):
        r0 = pl.multiple_of(i * R, R)
        st = st_ref[...]
        o_s[pl.ds(r0, R), :] += _nt(qs_s[pl.ds(r0, R), :].astype(BF16), st.astype(BF16))
        upd = _tn(vext[pl.ds(R + r0, R), :].astype(BF16), ks_s[pl.ds(r0, R), :].astype(BF16))
        st_ref[...] = st * g_s[pl.ds(r0, 1), :] + upd * mbd_ref[...]
        return carry

    lax.fori_loop(0, tm // R, body, 0)
    o = o_s[...]
    on = o * lax.rsqrt(_segmean(o * o, e256_ref) + EPS) * on_ref[...]
    og = og_ref[0]
    y_ref[0] = on * (og * jax.nn.sigmoid(og))

    @pl.when(t == pl.num_programs(1) - 1)
    def _():
        sfin_ref[0] = st_ref[...]


def _gla_call(qk, v, la, og, s0, lw):
    b, t, _ = qk.shape
    tp = -(-t // GLA_SUB) * GLA_SUB
    if tp != t:
        pad = lambda a: jnp.pad(a, ((0, 0), (0, tp - t), (0, 0)))
        qk, v, la, og = pad(qk), pad(v), pad(la), pad(og)
    tm = min(256, tp)
    kw, vw = GLA_HEADS * GLA_DK, GLA_HEADS * GLA_DV
    eye = jnp.eye(GLA_HEADS, dtype=F32)
    st0 = jnp.einsum('bhkv,hg->bhvgk', s0.astype(F32), eye).reshape(b, vw, kw)
    blk = lambda w: pl.BlockSpec((1, tm, w), lambda i, j: (i, j, 0))
    st_spec = pl.BlockSpec((1, vw, kw), lambda i, j: (i, 0, 0))
    consts = (lw['on'], lw['eh'], lw['mbd'], lw['e256'])
    y, st = pl.pallas_call(
        functools.partial(_gla_kernel, tm=tm), grid=(b, tp // tm),
        in_specs=[blk(2 * kw), blk(vw), blk(kw), blk(vw), st_spec] + [_const_spec(c.shape) for c in consts],
        out_specs=[blk(vw), st_spec],
        out_shape=[jax.ShapeDtypeStruct((b, tp, vw), F32), jax.ShapeDtypeStruct((b, vw, kw), F32)],
        scratch_shapes=[pltpu.VMEM((vw, kw), F32),
                        pltpu.VMEM((GLA_SUB + tm, kw), F32), pltpu.VMEM((GLA_SUB + tm, kw), F32),
                        pltpu.VMEM((GLA_SUB + tm, vw), F32),
                        pltpu.VMEM((tm, kw), F32), pltpu.VMEM((tm, kw), F32), pltpu.VMEM((tm, kw), F32),
                        pltpu.VMEM((tm, vw), F32)],
        compiler_params=_params(("arbitrary", "arbitrary")), name="gla",
    )(qk, v, la, og, st0, *consts)
    st = st.reshape(b, GLA_HEADS, GLA_DV, GLA_HEADS, GLA_DK)
    s_fin = jnp.stack([st[:, h, :, h, :] for h in range(GLA_HEADS)], axis=1).swapaxes(-1, -2)
    return y[:, :t], s_fin


def _cmp_copy(pt_ref, cache_ref, buf, sem, step, slot, j, kind, n_steps, page, feature_major):
    bb, ss = step // n_steps, step % n_steps
    pg = pt_ref[bb, ss * CMP_PAGES + j]
    src = (cache_ref.at[pg, pl.ds(kind * LANES, LANES), :] if feature_major
           else cache_ref.at[pg, :, pl.ds(kind * LANES, LANES)])
    return pltpu.make_async_copy(src, buf.at[slot, kind, pl.ds(j * page, page), :], sem.at[slot])


def _compress_kernel(pt_ref, cache_ref, wk_ref, wv_ref, pek_ref, pev_ref, kn0_ref, e128_ref,
                     kch_ref, kcl_ref, vct_ref, buf, sem, f_s, bias_s, *, n_steps, page, total, feature_major):
    copy = functools.partial(_cmp_copy, pt_ref, cache_ref, buf, sem, n_steps=n_steps, page=page,
                             feature_major=feature_major)
    bi, si = pl.program_id(0), pl.program_id(1)
    step = bi * n_steps + si
    slot = step % 2
    rows = CMP_PAGES * page
    nchunk = rows // CMP_STRIDE

    def start(st, sl):
        for j in range(CMP_PAGES):
            for kind in range(2):
                copy(st, sl, j, kind).start()

    @pl.when(step == 0)
    def _():
        start(step, slot)
        rid = lax.broadcasted_iota(jnp.int32, (CMP_STRIDE, 2 * LANES), 0)
        bk = jnp.zeros((CMP_STRIDE, 2 * LANES), F32)
        bv = jnp.zeros((CMP_STRIDE, 2 * LANES), F32)
        for l in range(CMP_STRIDE):
            pk, pv = pek_ref[...], pev_ref[...]
            w_k = wk_ref[l // 2][(l % 2) * LANES:(l % 2 + 1) * LANES, :]
            w_v = wv_ref[l // 2][(l % 2) * LANES:(l % 2 + 1) * LANES, :]
            tk = jnp.concatenate([_bdot(pk[:, :LANES], w_k[:, :LANES]), _bdot(pk[:, LANES:], w_k[:, LANES:])], axis=1)
            tv = jnp.concatenate([_bdot(pv[:, :LANES], w_v[:, :LANES]), _bdot(pv[:, LANES:], w_v[:, LANES:])], axis=1)
            bk = bk + jnp.where(rid == l, tk, 0.0)
            bv = bv + jnp.where(rid == l, tv, 0.0)
        bias_s[0:1, :] = jnp.sum(bk, axis=0, keepdims=True)
        bias_s[1:2, :] = jnp.sum(bv, axis=0, keepdims=True)

    @pl.when(step + 1 < total)
    def _():
        start(step + 1, 1 - slot)

    for j in range(CMP_PAGES):
        for kind in range(2):
            copy(step, slot, j, kind).wait()
    if feature_major:
        for j in range(CMP_PAGES):
            for kind in range(2):
                buf[slot, kind, j * page:(j + 1) * page, :] = buf[slot, kind, j * page:(j + 1) * page, :].T

    def rows_of(kind, l):
        return buf[slot, kind, pl.ds(l, nchunk, stride=CMP_STRIDE), :].astype(BF16)

    fk = jnp.zeros((nchunk, 2 * LANES), F32)
    fv = jnp.zeros((nchunk, 2 * LANES), F32)
    for l in range(0, CMP_STRIDE, 2):
        xk = jnp.concatenate([rows_of(0, l), rows_of(0, l + 1)], axis=1)
        xv = jnp.concatenate([rows_of(1, l), rows_of(1, l + 1)], axis=1)
        fk = fk + jnp.dot(xk, wk_ref[l // 2], preferred_element_type=F32)
        fv = fv + jnp.dot(xv, wv_ref[l // 2], preferred_element_type=F32)
    m0 = pl.multiple_of(si * nchunk, nchunk)
    f_s[pl.ds(m0, nchunk), 0:2 * LANES] = fk
    f_s[pl.ds(m0, nchunk), 2 * LANES:4 * LANES] = fv

    @pl.when(si == n_steps - 1)
    def _():
        ncp = f_s.shape[0]
        last = lax.broadcasted_iota(jnp.int32, (ncp, LANES), 0) == ncp - 1
        kc = f_s[:, 0:LANES] + pltpu.roll(f_s[:, LANES:2 * LANES], ncp - 1, axis=0)
        kc = kc + bias_s[0:1, 0:LANES] + bias_s[0:1, LANES:2 * LANES]
        kc = jnp.where(last, 0.0, kc)
        kc = kc * lax.rsqrt(_segmean(kc * kc, e128_ref) + EPS) * kn0_ref[...]
        hi, lo = _split(kc)
        kch_ref[0] = hi
        kcl_ref[0] = lo
        vc = f_s[:, 2 * LANES:3 * LANES] + pltpu.roll(f_s[:, 3 * LANES:4 * LANES], ncp - 1, axis=0)
        vc = vc + bias_s[1:2, 0:LANES] + bias_s[1:2, LANES:2 * LANES]
        vct_ref[0] = jnp.where(last, 0.0, vc).T.astype(BF16)


def _compress_call(cache, page_table, lw, feature_major):
    b, n_pages = page_table.shape
    page = cache.shape[2] if feature_major else cache.shape[1]
    assert n_pages % CMP_PAGES == 0 and page % CMP_STRIDE == 0 and (page == LANES or not feature_major)
    n_steps = n_pages // CMP_PAGES
    ncp = n_pages * page // CMP_STRIDE
    rows = CMP_PAGES * page
    consts = (lw['wk'], lw['wv'], lw['pek'], lw['pev'], lw['kn0'], lw['e128'])
    out3 = lambda s: pl.BlockSpec((1,) + s, lambda i, j, pt: (i, 0, 0))
    grid_spec = pltpu.PrefetchScalarGridSpec(
        num_scalar_prefetch=1, grid=(b, n_steps),
        in_specs=[pl.BlockSpec(memory_space=pl.ANY)]
        + [pl.BlockSpec(c.shape, lambda i, j, pt, nd=c.ndim: (0,) * nd) for c in consts],
        out_specs=[out3((ncp, LANES)), out3((ncp, LANES)), out3((LANES, ncp))],
        scratch_shapes=[pltpu.VMEM((2, 2, rows, LANES), F32), pltpu.SemaphoreType.DMA((2,)),
                        pltpu.VMEM((ncp, 4 * LANES), F32), pltpu.VMEM((SUBLANES, 2 * LANES), F32)])
    return pl.pallas_call(
        functools.partial(_compress_kernel, n_steps=n_steps, page=page, total=b * n_steps,
                          feature_major=feature_major),
        grid_spec=grid_spec,
        out_shape=[jax.ShapeDtypeStruct((b, ncp, LANES), BF16), jax.ShapeDtypeStruct((b, ncp, LANES), BF16),
                   jax.ShapeDtypeStruct((b, LANES, ncp), BF16)],
        compiler_params=_params(("arbitrary", "arbitrary")), name="compress",
    )(page_table, cache, *consts)


def _softmax_cols(s, mask):
    sm = jnp.where(mask, s, NEG)
    m = jnp.max(sm, axis=0, keepdims=True)
    p = jnp.where(mask, jnp.exp(sm - m), 0.0)
    return p, m, jnp.sum(p, axis=0, keepdims=True)


def _flash_update(carry, s, mask, vt):
    m, l, acc = carry
    sm = jnp.where(mask, s, NEG)
    m_new = jnp.maximum(m, jnp.max(sm, axis=0, keepdims=True))
    alpha = jnp.exp(m - m_new)
    p = jnp.where(mask, jnp.exp(sm - m_new), 0.0)
    l = alpha * l + jnp.sum(p, axis=0, keepdims=True)
    acc = acc * alpha + jnp.dot(vt, p.astype(BF16), preferred_element_type=F32)
    return m_new, l, acc


def _compressed_branch(kch, kcl, vct, qp, colpos):
    qh, ql = _split(qp)
    s = _nt(kch, qh) + _nt(kch, ql) + _nt(kcl, qh)
    n_idx = lax.broadcasted_iota(jnp.int32, s.shape, 0)
    mask = n_idx * CMP_STRIDE + (2 * CMP_STRIDE - 1) <= colpos
    p, _, l = _softmax_cols(s, mask)
    p = p / jnp.maximum(l, 1e-30)
    return p, jnp.dot(vct, p.astype(BF16), preferred_element_type=F32)


def _block_scores(mbt_ref, imp, colpos):
    ih, il = _split(imp)
    blk = jnp.dot(mbt_ref[...], ih, preferred_element_type=F32) + jnp.dot(mbt_ref[...], il, preferred_element_type=F32)
    j_idx = lax.broadcasted_iota(jnp.int32, blk.shape, 0)
    avail = j_idx * SEL_BLOCK <= colpos
    forced = (j_idx == colpos // SEL_BLOCK) | (j_idx == 0)
    return jnp.where(avail, jnp.where(forced, FORCE, blk), NEG)


def _pick_round(x, j_f):
    m = jnp.max(x, axis=0, keepdims=True)
    first = jnp.min(jnp.where(x == m, j_f, 1e9), axis=0, keepdims=True)
    pick = j_f == first
    return m, first, pick, jnp.where(pick, -jnp.inf, x)


def _head_rows_to_lanes(o_rows, gates, rows_per_head):
    n = rows_per_head
    lo_half = lax.broadcasted_iota(jnp.int32, (n, LANES), 1) < NSA_HD
    outs = []
    for h in range(NSA_HEADS):
        acc = None
        for c, o in enumerate(o_rows):
            term = gates[:, 3 * h + c:3 * h + c + 1] * o[h * n:(h + 1) * n, :]
            acc = term if acc is None else acc + term
        outs.append(acc)
    y01 = jnp.where(lo_half, outs[0], pltpu.roll(outs[1], NSA_HD, axis=1))
    y23 = jnp.where(lo_half, pltpu.roll(outs[2], NSA_HD, axis=1), outs[3])
    return jnp.concatenate([y01, y23], axis=1)


def _nsa_prompt_kernel(q_ref, ng_ref, kch_ref, kcl_ref, vct_ref, mbt_ref, ks_ref, svt_ref, kw_ref, wvt_ref,
                       e16_ref, y_ref, sel_s):
    qb_rows = Q_BLOCK
    s0 = pl.program_id(1) * qb_rows
    q = q_ref[0]
    lo_half = lax.broadcasted_iota(jnp.int32, (qb_rows, LANES), 1) < NSA_HD
    q01, q23 = q[:, :LANES], q[:, LANES:]
    qp = jnp.concatenate([jnp.where(lo_half, q01, 0.0),
                          jnp.where(lo_half, pltpu.roll(q01, NSA_HD, axis=1), 0.0),
                          jnp.where(lo_half, 0.0, pltpu.roll(q23, NSA_HD, axis=1)),
                          jnp.where(lo_half, 0.0, q23)], axis=0)
    ncols = NSA_HEADS * qb_rows
    colpos = s0 + lax.broadcasted_iota(jnp.int32, (1, ncols), 1) % qb_rows
    qb = qp.astype(BF16)

    p, oc_t = _compressed_branch(kch_ref[0], kcl_ref[0], vct_ref[0], qp, colpos)
    imp = jnp.concatenate([p[:, 0:qb_rows] + p[:, qb_rows:2 * qb_rows],
                           p[:, 2 * qb_rows:3 * qb_rows] + p[:, 3 * qb_rows:]], axis=1)
    x = _block_scores(mbt_ref, imp, colpos[:, :2 * qb_rows])
    j_f = lax.broadcasted_iota(jnp.int32, x.shape, 0).astype(F32)
    sel = jnp.zeros(x.shape, F32)
    sel = jnp.full(x.shape, NEG, F32)
    for _ in range(N_SEL):
        m, _, pick, x = _pick_round(x, j_f)
        sel = jnp.where(pick & (m > 0.5 * NEG), 0.0, sel)
    sel_s[...] = jnp.concatenate([sel[:, :qb_rows], sel[:, :qb_rows], sel[:, qb_rows:], sel[:, qb_rows:]], axis=1)

    kt = SEL_TILE
    nblk = kt // SEL_BLOCK
    qpt = qp.T.astype(BF16)
    pad_rows = jnp.zeros((LANES - nblk, ncols), BF16)

    def scores(c):
        k0 = pl.multiple_of(c * kt, kt)
        keys = jnp.concatenate([ks_ref[0, pl.ds(k0, kt), :], e16_ref[...]], axis=1)
        bias = sel_s[pl.ds(pl.multiple_of(c * nblk, nblk), nblk), :].astype(BF16)
        return k0, jnp.dot(keys, jnp.concatenate([qpt, bias, pad_rows], axis=0), preferred_element_type=F32)

    def body(c, carry):
        m, l, acc = carry
        k0, s = scores(c)
        m_new = jnp.maximum(m, jnp.max(s, axis=0, keepdims=True))
        alpha = jnp.exp(m - m_new)
        p = jnp.exp(s - m_new)
        l = alpha * l + jnp.sum(p, axis=0, keepdims=True)
        acc = acc * alpha + jnp.dot(svt_ref[0, :, pl.ds(k0, kt)], p.astype(BF16), preferred_element_type=F32)
        return m_new, l, acc

    init = (jnp.full((1, ncols), NEG, F32), jnp.zeros((1, ncols), F32), jnp.zeros((LANES, ncols), F32))
    last = (s0 + qb_rows - 1) // kt
    carry = lax.fori_loop(0, last, body, init)
    k0, s = scores(last)
    kpos = k0 + lax.broadcasted_iota(jnp.int32, s.shape, 0)
    _, l, acc = _flash_update(carry, s, kpos <= colpos, svt_ref[0, :, pl.ds(k0, kt)])
    os_t = acc / jnp.maximum(l, 1e-30)

    span = WINDOW + qb_rows
    w0 = pl.multiple_of(jnp.maximum(s0 - WINDOW, 0), qb_rows)
    s = _nt(kw_ref[0, pl.ds(w0, span), :], qb)
    dpos = colpos - (w0 + lax.broadcasted_iota(jnp.int32, s.shape, 0))
    pw, _, lw_ = _softmax_cols(s, (dpos >= 0) & (dpos < WINDOW))
    ow_t = jnp.dot(wvt_ref[0, :, pl.ds(w0, span)], pw.astype(BF16), preferred_element_type=F32) / jnp.maximum(lw_, 1e-30)

    y_ref[0] = _head_rows_to_lanes([oc_t.T, os_t.T, ow_t.T], ng_ref[0], qb_rows)


def _nsa_prompt_call(q, ng, kch, kcl, vct, ksb, svt, kwb, wvt, lw):
    b, t, _ = q.shape
    assert t % SEL_TILE == 0 and t >= WINDOW + Q_BLOCK
    ncp = kch.shape[1]
    nbp = t // SEL_BLOCK
    full = lambda s: pl.BlockSpec((1,) + s, lambda i, j: (i, 0, 0))
    blk = lambda w: pl.BlockSpec((1, Q_BLOCK, w), lambda i, j: (i, j, 0))
    return pl.pallas_call(
        _nsa_prompt_kernel, grid=(b, t // Q_BLOCK),
        in_specs=[blk(2 * LANES), blk(LANES), full((ncp, LANES)), full((ncp, LANES)), full((LANES, ncp)),
                  _const_spec(lw['mbt_p'].shape), full((t, LANES)), full((LANES, t)), full((t, LANES)),
                  full((LANES, t)), _const_spec(lw['e16'].shape)],
        out_specs=blk(2 * LANES), out_shape=jax.ShapeDtypeStruct((b, t, 2 * LANES), F32),
        scratch_shapes=[pltpu.VMEM((nbp, NSA_HEADS * Q_BLOCK), F32)],
        compiler_params=_params(("arbitrary", "arbitrary")), name="nsa_prompt",
    )(q, ng, kch, kcl, vct, lw['mbt_p'], ksb, svt, kwb, wvt, lw['e16'])


_SCOL = SUBLANES


def _nsa_sample_a_kernel(qp_ref, kch_ref, kcl_ref, vct_ref, mbt_ref, cwin_ref, wnew_ref,
                         oc_ref, ow_ref, idx_ref, ok_ref, *, past_len, n_dec):
    qp = qp_ref[0]
    qb = qp.astype(BF16)
    col = lax.broadcasted_iota(jnp.int32, (1, LANES), 1)
    tcol = col % _SCOL
    colpos = past_len + tcol
    p, oc_t = _compressed_branch(kch_ref[0], kcl_ref[0], vct_ref[0], qp, colpos)
    oc_ref[0] = oc_t
    first_head = (col % (2 * _SCOL)) < _SCOL
    imp = p + jnp.where(first_head, pltpu.roll(p, LANES - _SCOL, axis=1), pltpu.roll(p, _SCOL, axis=1))
    x = _block_scores(mbt_ref, imp, colpos)
    j_f = lax.broadcasted_iota(jnp.int32, x.shape, 0).astype(F32)
    for rd in range(N_SEL - 1):
        m, first, _, x = _pick_round(x, j_f)
        idx_ref[0, rd:rd + 1, :] = first.astype(jnp.int32)
        ok_ref[0, rd:rd + 1, :] = jnp.where(m > 0.5 * NEG, 1.0, 0.0)
    idx_ref[0, N_SEL - 1:N_SEL, :] = jnp.zeros((1, LANES), jnp.int32)
    ok_ref[0, N_SEL - 1:N_SEL, :] = jnp.ones((1, LANES), F32)

    kp, kn = cwin_ref[0], wnew_ref[0]
    wbuf = kp.shape[0]
    s1 = _nt(kp[:, :LANES].astype(BF16), qb)
    kpos1 = past_len - wbuf + lax.broadcasted_iota(jnp.int32, s1.shape, 0)
    d1 = colpos - kpos1
    mask1 = (d1 >= 0) & (d1 < WINDOW) & (kpos1 >= 0)
    s2 = _nt(kn[:, :LANES].astype(BF16), qb)
    kidx = lax.broadcasted_iota(jnp.int32, s2.shape, 0)
    mask2 = (kidx <= tcol) & (kidx < n_dec)
    carry = (jnp.full((1, LANES), NEG, F32), jnp.zeros((1, LANES), F32), jnp.zeros((LANES, LANES), F32))
    carry = _flash_update(carry, s1, mask1, kp[:, LANES:].T.astype(BF16))
    _, l, acc = _flash_update(carry, s2, mask2, kn[:, LANES:].T.astype(BF16))
    ow_ref[0] = acc / jnp.maximum(l, 1e-30)


def _nsa_sample_a_call(qp, kch, kcl, vct, cwin, wnew, lw, past_len, n_dec):
    b = qp.shape[0]
    ncp = kch.shape[1]
    full = lambda a: pl.BlockSpec((1,) + a.shape[1:], lambda i: (i, 0, 0))
    sq = pl.BlockSpec((1, LANES, LANES), lambda i: (i, 0, 0))
    rnd = pl.BlockSpec((1, N_SEL, LANES), lambda i: (i, 0, 0))
    return pl.pallas_call(
        functools.partial(_nsa_sample_a_kernel, past_len=past_len, n_dec=n_dec), grid=(b,),
        in_specs=[sq, full(kch), full(kcl), full(vct), _const_spec(lw['mbt_s'].shape), full(cwin), full(wnew)],
        out_specs=[sq, sq, rnd, rnd],
        out_shape=[jax.ShapeDtypeStruct((b, LANES, LANES), F32)] * 2
        + [jax.ShapeDtypeStruct((b, N_SEL, LANES), jnp.int32), jax.ShapeDtypeStruct((b, N_SEL, LANES), F32)],
        compiler_params=_params(("parallel",)), name="nsa_sample_a",
    )(qp, kch, kcl, vct, lw['mbt_s'], cwin, wnew)


def _sel_copy(pt_ref, info_ref, cache_ref, bufs, sem, bi, slot, c, rd, kind, n_dec, page):
    j = info_ref[bi, c * (N_SEL - 1) + rd]
    pg = pt_ref[bi, j // (page // SEL_BLOCK)]
    feat = (2 + kind) * LANES + (c // n_dec) * NSA_HD
    return pltpu.make_async_copy(cache_ref.at[pg, pl.ds(feat, NSA_HD), :],
                                 bufs[kind].at[slot, c, :, pl.ds(rd * page, page)], sem.at[slot])


def _nsa_sample_b_kernel(pt_ref, info_ref, cache_ref, qpt_ref, newt_ref, oc_ref, ow_ref, ngt_ref,
                         y_ref, kbuf, vbuf, sem, *, n_dec, page):
    bi = pl.program_id(0)
    slot = bi % 2
    ncomb = NSA_KV * n_dec
    nsel = N_SEL - 1
    past_keys = nsel * page
    copy = functools.partial(_sel_copy, pt_ref, info_ref, cache_ref, (kbuf, vbuf), sem, n_dec=n_dec, page=page)

    def start(b, sl):
        for c in range(ncomb):
            for rd in range(nsel):
                for kind in range(2):
                    copy(b, sl, c, rd, kind).start()

    @pl.when(bi == 0)
    def _():
        start(bi, slot)

    @pl.when(bi + 1 < pl.num_programs(0))
    def _():
        start(bi + 1, 1 - slot)

    newt = newt_ref[0]
    for c in range(ncomb):
        g = c // n_dec
        kbuf[slot, c, :, past_keys:past_keys + LANES] = newt[g * NSA_HD:(g + 1) * NSA_HD, :]
        vbuf[slot, c, :, past_keys:past_keys + LANES] = newt[LANES + g * NSA_HD:LANES + (g + 1) * NSA_HD, :]
    for c in range(ncomb):
        for rd in range(nsel):
            for kind in range(2):
                copy(bi, slot, c, rd, kind).wait()

    qpt = qpt_ref[0]
    lane = lax.broadcasted_iota(jnp.int32, (1, LANES), 1)
    lane_p = lax.broadcasted_iota(jnp.int32, (1, page), 1)
    rep = NSA_HEADS // NSA_KV
    os_h = [jnp.zeros((NSA_HD, LANES), F32) for _ in range(NSA_HEADS)]
    for c in range(ncomb):
        g, t = divmod(c, n_dec)
        pieces = []
        for rd in range(nsel):
            j = info_ref[bi, c * nsel + rd]
            ok = info_ref[bi, ncomb * nsel + c * nsel + rd]
            chosen = (lane_p // SEL_BLOCK == j % (page // SEL_BLOCK)) & (ok > 0)
            pieces.append(jnp.where(chosen, 0.0, NEG))
        pieces.append(jnp.where(lane <= t, 0.0, NEG))
        bias = jnp.concatenate(pieces, axis=1)
        keys, vals = kbuf[slot, c], vbuf[slot, c]
        for r in range(rep):
            h = g * rep + r
            cidx = h * _SCOL + t
            qcol = qpt[g * NSA_HD:(g + 1) * NSA_HD, cidx:cidx + 1]
            s = jnp.sum(keys * qcol, axis=0, keepdims=True) + bias
            m = jnp.max(s, axis=1, keepdims=True)
            p = jnp.exp(s - m)
            l = jnp.sum(p, axis=1, keepdims=True)
            o = jnp.sum(vals * p, axis=1, keepdims=True) / jnp.maximum(l, 1e-30)
            os_h[h] = jnp.where(lane == t, o, os_h[h])
    oc, ow, ngt = oc_ref[0], ow_ref[0], ngt_ref[0]
    for h in range(NSA_HEADS):
        g = h // rep

        def to_front(a):
            blk = a[g * NSA_HD:(g + 1) * NSA_HD, :]
            return blk if h == 0 else pltpu.roll(blk, LANES - h * _SCOL, axis=1)

        y_ref[0, h * NSA_HD:(h + 1) * NSA_HD, :] = (ngt[3 * h:3 * h + 1, :] * to_front(oc)
                                                   + ngt[3 * h + 1:3 * h + 2, :] * os_h[h]
                                                   + ngt[3 * h + 2:3 * h + 3, :] * to_front(ow))


def _nsa_sample_b_call(page_table, info, cache_t, qpt, newt, oc, ow, ngt, n_dec):
    b = qpt.shape[0]
    page = cache_t.shape[2]
    ncomb = NSA_KV * n_dec
    keys = (N_SEL - 1) * page + LANES
    spec = lambda a: pl.BlockSpec((1,) + a.shape[1:], lambda i, pt, sl: (i, 0, 0))
    grid_spec = pltpu.PrefetchScalarGridSpec(
        num_scalar_prefetch=2, grid=(b,),
        in_specs=[pl.BlockSpec(memory_space=pl.ANY), spec(qpt), spec(newt), spec(oc), spec(ow), spec(ngt)],
        out_specs=pl.BlockSpec((1, NSA_HEADS * NSA_HD, LANES), lambda i, pt, sl: (i, 0, 0)),
        scratch_shapes=[pltpu.VMEM((2, ncomb, NSA_HD, keys), F32), pltpu.VMEM((2, ncomb, NSA_HD, keys), F32),
                        pltpu.SemaphoreType.DMA((2,))])
    return pl.pallas_call(
        functools.partial(_nsa_sample_b_kernel, n_dec=n_dec, page=page), grid_spec=grid_spec,
        out_shape=jax.ShapeDtypeStruct((b, NSA_HEADS * NSA_HD, LANES), F32),
        compiler_params=_params(("arbitrary",)), name="nsa_sample_b",
    )(page_table, info, cache_t, qpt, newt, oc, ow, ngt)


def _merge_kernel(x_ref, g1_ref, wg_ref, ya_ref, yb_ref, yc_ref, yd_ref, wout_ref, wo_ref, o_ref):
    x = x_ref[...]
    h = _rms(x, g1_ref[...]).astype(BF16)
    mix = None
    for i, y_ref in enumerate((ya_ref, yb_ref, yc_ref, yd_ref)):
        gate = jax.nn.sigmoid(_nt(h, wg_ref[i * D_MODEL:(i + 1) * D_MODEL, :]))
        term = gate * _bdot(y_ref[...], wout_ref[i])
        mix = term if mix is None else mix + term
    o_ref[...] = x + _bdot(mix, wo_ref[...])


def _merge_call(x, ys, lw):
    n = x.shape[0]
    tm = min(512, n)
    row = lambda w: pl.BlockSpec((tm, w), lambda i: (i, 0))
    return pl.pallas_call(
        _merge_kernel, grid=(n // tm,),
        in_specs=[row(D_MODEL), _const_spec(lw['g1'].shape), _const_spec(lw['wgate'].shape)] + [row(2 * LANES)] * 4
        + [_const_spec(lw['wout'].shape), _const_spec(lw['wo'].shape)],
        out_specs=row(D_MODEL), out_shape=jax.ShapeDtypeStruct((n, D_MODEL), F32),
        compiler_params=_params(("parallel",)), name="merge",
    )(x, lw['g1'], lw['wgate'], *ys, lw['wout'], lw['wo'])


_FFN_CHUNK = 256


def _ffn_kernel(x_ref, g2_ref, wg_ref, wu_ref, wd_ref, o_ref):
    x = x_ref[...]
    h = _rms(x, g2_ref[...]).astype(BF16)
    acc = x
    for c in range(0, D_FF, _FFN_CHUNK):
        g = jnp.dot(h, wg_ref[:, c:c + _FFN_CHUNK], preferred_element_type=F32)
        u = jnp.dot(h, wu_ref[:, c:c + _FFN_CHUNK], preferred_element_type=F32)
        acc = acc + _bdot(g * jax.nn.sigmoid(g) * u, wd_ref[c:c + _FFN_CHUNK, :])
    o_ref[...] = acc


def _ffn_call(x, lw):
    n = x.shape[0]
    tm = min(512, n)
    row = pl.BlockSpec((tm, D_MODEL), lambda i: (i, 0))
    return pl.pallas_call(
        _ffn_kernel, grid=(n // tm,),
        in_specs=[row, _const_spec(lw['g2'].shape), _const_spec(lw['fg'].shape), _const_spec(lw['fu'].shape),
                  _const_spec(lw['fd'].shape)],
        out_specs=row, out_shape=jax.ShapeDtypeStruct((n, D_MODEL), F32),
        compiler_params=_params(("parallel",)), name="ffn",
    )(x, lw['g2'], lw['fg'], lw['fu'], lw['fd'])


def _block_diag(blocks):
    g, a, b = blocks.shape[-3:]
    eye = jnp.eye(g, dtype=blocks.dtype)
    out = jnp.einsum('...gab,gh->...gahb', blocks, eye)
    return out.reshape(blocks.shape[:-3] + (g * a, g * b))


def _seg_mean_matrix(width):
    seg = jnp.arange(width) // NSA_HD
    return jnp.where(seg[:, None] == seg[None, :], 1.0 / NSA_HD, 0.0).astype(BF16)


def _band_matrix(nbp, ncp):
    ratio = SEL_BLOCK // CMP_STRIDE
    j = jnp.arange(nbp)[:, None]
    n = jnp.arange(ncp)[None, :]
    return ((n >= ratio * j - 1) & (n <= ratio * j + ratio - 1)).astype(BF16)


def _expand_matrix(nrows, nblk):
    return (jnp.arange(nrows)[:, None] // SEL_BLOCK == jnp.arange(nblk)[None, :]).astype(BF16)


def _prep_layer(w, l, t_prompt, past_len):
    row = lambda a: a.reshape(1, -1).astype(F32)
    w_in_t = jnp.transpose(w['w_in'], (2, 0, 1))[:, l, :]
    pad_to = lambda a, n: jnp.pad(a, ((0, n - a.shape[0]), (0, 0)))
    wp = jnp.concatenate([w_in_t[0:1280], pad_to(w_in_t[1280:1296], LANES), w_in_t[1296:2576],
                          pad_to(w_in_t[2576:2588], LANES)], axis=0).astype(BF16)
    phi, pe = w['nsa_phi'][l], w['nsa_pe'][l]
    half = CMP_STRIDE

    def cmp_w(p):
        bd = lambda a: _block_diag(jnp.broadcast_to(a[:, None], (half, NSA_KV) + a.shape[1:]))
        per_l = jnp.concatenate([bd(p[:half]), bd(p[half:])], axis=-1)
        return per_l.reshape(half // 2, 2 * LANES, 2 * LANES).astype(BF16)

    def cmp_pe(p):
        return jnp.concatenate([jnp.tile(p[:half], (1, NSA_KV)), jnp.tile(p[half:], (1, NSA_KV))], axis=1).astype(F32)

    kw = GLA_HEADS * GLA_DK
    hk = jnp.arange(kw) // GLA_DK
    hv = jnp.arange(GLA_HEADS * GLA_DV) // GLA_DV
    same_head = hk[:, None] == hv[None, :]
    return {
        'g1': row(w['norm1'][l]), 'g2': row(w['norm2'][l]), 'wp': wp, 'wgate': w_in_t[2588:].astype(BF16),
        'wa2': jnp.pad(w['gla_wa2'][l], ((0, LANES - GLA_LOWRANK), (0, 0))).astype(BF16), 'ba': row(w['gla_ba'][l]),
        'qn': row(jnp.tile(w['nsa_qn'][l], NSA_HEADS)) * (NSA_HD ** -0.5),
        'kn0': row(jnp.tile(w['nsa_kn'][l][0], NSA_KV)), 'kn1': row(jnp.tile(w['nsa_kn'][l][1], NSA_KV)),
        'kn2': row(jnp.tile(w['nsa_kn'][l][2], NSA_KV)),
        'gb': row(jnp.pad(w['nsa_gb'][l], (0, LANES - 3 * NSA_HEADS))),
        'e256': _seg_mean_matrix(2 * LANES), 'e128': _seg_mean_matrix(LANES),
        'cw': jnp.pad(w['conv_w'][l], ((0, 1), (0, 0))).astype(F32), 'cb': row(w['conv_b'][l]),
        'clg': row(w['conv_ln_g'][l]), 'clb': row(w['conv_ln_b'][l]),
        'pw': _block_diag(w['pool_w'][l]).astype(BF16), 'psc': row(w['pool_scale'][l]),
        'on': row(jnp.tile(w['gla_onorm'][l], GLA_HEADS)),
        'eh': same_head.astype(BF16), 'mbd': same_head.T.astype(F32),
        'wk': cmp_w(phi[0]), 'wv': cmp_w(phi[1]), 'pek': cmp_pe(pe[0]), 'pev': cmp_pe(pe[1]),
        'mbt_p': _band_matrix(t_prompt // SEL_BLOCK, t_prompt // CMP_STRIDE),
        'mbt_s': _band_matrix(past_len // SEL_BLOCK, past_len // CMP_STRIDE),
        'e16': _expand_matrix(SEL_TILE, LANES),
        'wout': jnp.stack([w['w_out_conv'][l], w['w_out_gla'][l], w['w_out_pool'][l], w['w_out_nsa'][l]]).astype(BF16),
        'wo': w['w_o'][l].astype(BF16),
        'fg': w['ffn_gate'][l].astype(BF16), 'fu': w['ffn_up'][l].astype(BF16), 'fd': w['ffn_down'][l].astype(BF16),
    }


def _prompt_layer(x, lw):
    b, t, _ = x.shape
    xf = x.reshape(b * t, D_MODEL)
    a, qk, v, og, la, up, q, rows, win, ng, ksb, svt, kwb, wvt = _proj_call(xf, lw, True)
    r3 = lambda z: z.reshape(b, t, z.shape[-1])
    ya, u = _conv_call(r3(a), jnp.zeros((b, CONV_WIDTH - 1, CONV_CH), F32), lw)
    yc = _pool_call(r3(up), jnp.zeros((b, POOL_STATE, POOL_CH), F32), 0, lw)
    yb, gla_state = _gla_call(r3(qk), r3(v), r3(la), r3(og), jnp.zeros((b, GLA_HEADS, GLA_DK, GLA_DV), F32), lw)
    page = 2 * SEL_BLOCK
    cache = rows.reshape(b * t // page, page, rows.shape[-1])
    table = jnp.arange(b * t // page, dtype=jnp.int32).reshape(b, t // page)
    kch, kcl, vct = _compress_call(cache, table, lw, False)
    tcols =lambda z: z.reshape(LANES, b, t).swapaxes(0, 1)
    yd = _nsa_prompt_call(r3(q), r3(ng), kch, kcl, vct, r3(ksb), tcols(svt), r3(kwb), tcols(wvt), lw)
    x1 = _merge_call(xf, [ya.reshape(b * t, -1), yb.reshape(b * t, -1), yc.reshape(b * t, -1), yd.reshape(b * t, -1)], lw)
    x2 = _ffn_call(x1, lw).reshape(b, t, D_MODEL)
    keep = min(WINDOW, t)
    return (x2, r3(rows).reshape(b, t, 4, NSA_KV, NSA_HD), r3(win)[:, -keep:].reshape(b, keep, 2, NSA_KV, NSA_HD),
            u[:, -(CONV_WIDTH - 1):], r3(up)[:, -POOL_STATE:], gla_state)


def _sample_layer(x, lw, cache_t, cwin, conv_prev, pool_prev, gla_prev, page_table):
    b, t, _ = x.shape
    page = cache_t.shape[2]
    past_len = page_table.shape[1] * page
    assert past_len % SEL_BLOCK == 0 and t <= _SCOL and page % SEL_BLOCK == 0
    xf = x.reshape(b * t, D_MODEL)
    a, qk, v, og, la, up, q, rows, win, ng = _proj_call(xf, lw, False)
    r3 = lambda z: z.reshape(b, t, z.shape[-1])
    ya, u = _conv_call(r3(a), conv_prev, lw)
    yc = _pool_call(r3(up), pool_prev, past_len, lw)
    yb, gla_state = _gla_call(r3(qk), r3(v), r3(la), r3(og), gla_prev, lw)
    kch, kcl, vct = _compress_call(cache_t, page_table, lw, True)
    q4 = jnp.pad(r3(q).reshape(b, t, NSA_HEADS, NSA_HD), ((0, 0), (0, _SCOL - t), (0, 0), (0, 0))).swapaxes(1, 2)
    grp = jnp.arange(NSA_HEADS) // (NSA_HEADS // NSA_KV)
    onehot = (grp[:, None] == jnp.arange(NSA_KV)[None, :]).astype(F32)
    qp = jnp.einsum('bhtd,hg->bhtgd', q4, onehot).reshape(b, NSA_HEADS * _SCOL, LANES)
    qp = jnp.pad(qp, ((0, 0), (0, LANES - NSA_HEADS * _SCOL), (0, 0)))
    pad_rows = lambda z, n: jnp.pad(z, ((0, 0), (0, n - z.shape[1]), (0, 0)))
    cwin2 = cwin.reshape(b, cwin.shape[1], -1)
    oc, ow, idx, ok = _nsa_sample_a_call(qp, kch, kcl, vct, cwin2, pad_rows(r3(win), LANES), lw, past_len, t)
    cols = (jnp.arange(NSA_KV)[:, None] * (NSA_HEADS // NSA_KV) * _SCOL + jnp.arange(t)[None, :]).reshape(-1)
    per_comb = lambda z: z[:, :N_SEL - 1, :][:, :, cols].swapaxes(1, 2).reshape(b, -1)
    info = jnp.concatenate([per_comb(idx), per_comb(ok).astype(jnp.int32)], axis=1)
    lanes_t = lambda z, n: jnp.pad(z.swapaxes(1, 2), ((0, 0), (0, n - z.shape[2]), (0, LANES - z.shape[1])))
    yt = _nsa_sample_b_call(page_table, info, cache_t, qp.swapaxes(1, 2), lanes_t(r3(rows)[:, :, 2 * LANES:], 2 * LANES),
                            oc, ow, lanes_t(r3(ng)[:, :, :3 * NSA_HEADS], 2 * SUBLANES), t)
    yd = yt[:, :, :t].swapaxes(1, 2)
    x1 = _merge_call(xf, [ya.reshape(b * t, -1), yb.reshape(b * t, -1), yc.reshape(b * t, -1), yd.reshape(b * t, -1)], lw)
    x2 = _ffn_call(x1, lw).reshape(b, t, D_MODEL)
    win_all = jnp.concatenate([cwin2, r3(win)], axis=1)
    keep = min(WINDOW, win_all.shape[1])
    return (x2, r3(rows).reshape(b, t, 4, NSA_KV, NSA_HD), win_all[:, -keep:].reshape(b, keep, 2, NSA_KV, NSA_HD),
            jnp.concatenate([conv_prev, u], axis=1)[:, -(CONV_WIDTH - 1):],
            jnp.concatenate([pool_prev, r3(up)], axis=1)[:, -POOL_STATE:], gla_state)


def kernel(x_prompt, x_sample, cache_nsa_kv, cache_win_kv, state_conv, state_pool, state_gla, page_table, norm1, w_in, conv_w, conv_b, conv_ln_g, conv_ln_b, w_out_conv, gla_wa2, gla_ba, gla_onorm, w_out_gla, pool_w, pool_scale, w_out_pool, nsa_qn, nsa_kn, nsa_pe, nsa_phi, nsa_gb, w_out_nsa, w_o, norm2, ffn_gate, ffn_up, ffn_down):
    w = dict(norm1=norm1, w_in=w_in, conv_w=conv_w, conv_b=conv_b, conv_ln_g=conv_ln_g, conv_ln_b=conv_ln_b,
             w_out_conv=w_out_conv, gla_wa2=gla_wa2, gla_ba=gla_ba, gla_onorm=gla_onorm, w_out_gla=w_out_gla,
             pool_w=pool_w, pool_scale=pool_scale, w_out_pool=w_out_pool, nsa_qn=nsa_qn, nsa_kn=nsa_kn,
             nsa_pe=nsa_pe, nsa_phi=nsa_phi, nsa_gb=nsa_gb, w_out_nsa=w_out_nsa, w_o=w_o, norm2=norm2,
             ffn_gate=ffn_gate, ffn_up=ffn_up, ffn_down=ffn_down)
    depth, n_pool, page = cache_nsa_kv.shape[:3]
    past_len = page_table.shape[1] * page
    cache = jnp.moveaxis(cache_nsa_kv, 2, -1).reshape(depth * n_pool, -1, page)
    xp, xs = x_prompt, x_sample
    outs_p, outs_s = [], []
    for l in range(depth):
        lw = _prep_layer(w, l, xp.shape[1], past_len)
        xp, *st_p = _prompt_layer(xp, lw)
        xs, *st_s = _sample_layer(xs, lw, cache, cache_win_kv[l], state_conv[l], state_pool[l],
                                  state_gla[l], page_table + l * n_pool)
        outs_p.append(st_p)
        outs_s.append(st_s)
    stack = lambda outs, i: jnp.stack([o[i] for o in outs])
    return (xp, xs) + tuple(stack(outs_p, i) for i in range(5)) + tuple(stack(outs_s, i) for i in range(5))
```

```python
import functools

import numpy as np
import jax
import jax.numpy as jnp
from jax import lax
from jax.experimental import pallas as pl
from jax.experimental.pallas import tpu as pltpu

F32, BF16 = jnp.float32, jnp.bfloat16

D_MODEL = 1024
CONV_CH = 256
CONV_WIDTH = 31
GLA_HEADS, GLA_DK, GLA_DV = 4, 32, 64
GLA_LOWRANK = 16
GLA_TAU = 16.0
POOL_CH = 256
POOL_WINDOWS = (2, 4, 8, 16)
POOL_STATE = 15
NSA_HEADS, NSA_HD, NSA_KV = 4, 64, 2
CMP_STRIDE = 16
SEL_BLOCK = 64
N_SEL = 16
WINDOW = 512
Q_BLOCK = 128
D_FF = 2816
EPS = 1e-6
NEG = -1e30
FORCE = 1e4

LANES = 128
SUBLANES = 8
VMEM_LIMIT = 56 * 1024 * 1024

GLA_SUB = 16
SEL_TILE = 1024
CMP_PAGES = 16

_P_A, _P_QK, _P_V, _P_O, _P_LR, _P_UP, _P_NQ, _P_KV, _P_NG, _P_END = (
    0, 512, 768, 1024, 1280, 1408, 1664, 1920, 2688, 2816)


def _bdot(a, b):
    return jnp.dot(a.astype(BF16), b.astype(BF16), preferred_element_type=F32)


def _nt(a, b):
    return lax.dot_general(a, b, (((1,), (1,)), ((), ())), preferred_element_type=F32)


def _tn(a, b):
    return lax.dot_general(a, b, (((0,), (0,)), ((), ())), preferred_element_type=F32)


def _split(x):
    hi = x.astype(BF16)
    return hi, (x - hi.astype(F32)).astype(BF16)


def _rms(x, g):
    return x * lax.rsqrt(jnp.mean(x * x, axis=-1, keepdims=True) + EPS) * g


def _segmean(x2, e_ref):
    hi, lo = _split(x2)
    e = e_ref[...]
    return jnp.dot(hi, e, preferred_element_type=F32) + jnp.dot(lo, e, preferred_element_type=F32)


def _const_spec(shape):
    nd = len(shape)
    return pl.BlockSpec(shape, lambda *_: (0,) * nd)


def _params(sem):
    return pltpu.CompilerParams(dimension_semantics=sem, vmem_limit_bytes=VMEM_LIMIT)


def _proj_kernel(x_ref, g1_ref, wp_ref, wa2_ref, ba_ref, qn_ref, kn1_ref, kn2_ref, gb_ref, e256_ref, e128_ref,
                 a_ref, qk_ref, v_ref, o_ref, la_ref, up_ref, q_ref, rows_ref, win_ref, ng_ref, *t_refs):
    h = _rms(x_ref[...], g1_ref[...]).astype(BF16)

    def seg(a, b):
        return _nt(h, wp_ref[a:b, :])

    a_ref[...] = seg(_P_A, _P_QK)
    qk_ref[...] = seg(_P_QK, _P_V)
    v_ref[...] = seg(_P_V, _P_O)
    o_ref[...] = seg(_P_O, _P_LR)
    z = _bdot(seg(_P_LR, _P_UP), wa2_ref[...]) + ba_ref[...]
    la_ref[...] = (jnp.minimum(z, 0.0) - jnp.log(1.0 + jnp.exp(-jnp.abs(z)))) * (1.0 / GLA_TAU)
    up_ref[...] = seg(_P_UP, _P_NQ)
    qr = seg(_P_NQ, _P_KV)
    q_ref[...] = qr * lax.rsqrt(_segmean(qr * qr, e256_ref) + EPS) * qn_ref[...]
    kv = seg(_P_KV, _P_NG)
    k_sel = kv[:, 256:384]
    k_sel = k_sel * lax.rsqrt(_segmean(k_sel * k_sel, e128_ref) + EPS) * kn1_ref[...]
    k_win = kv[:, 512:640]
    k_win = k_win * lax.rsqrt(_segmean(k_win * k_win, e128_ref) + EPS) * kn2_ref[...]
    v_sel, v_win = kv[:, 384:512], kv[:, 640:768]
    rows_ref[:, 0:256] = kv[:, 0:256]
    rows_ref[:, 256:384] = k_sel
    rows_ref[:, 384:512] = v_sel
    win_ref[:, 0:128] = k_win
    win_ref[:, 128:256] = v_win
    ng_ref[...] = jax.nn.sigmoid(seg(_P_NG, _P_END) + gb_ref[...])
    if t_refs:
        ksb_ref, svt_ref, kwb_ref, wvt_ref = t_refs
        ksb_ref[...] = k_sel.astype(BF16)
        svt_ref[...] = v_sel.T.astype(BF16)
        kwb_ref[...] = k_win.astype(BF16)
        wvt_ref[...] = v_win.T.astype(BF16)


def _proj_call(x, lw, with_t):
    n = x.shape[0]
    tm = min(512, n)
    row = lambda w: pl.BlockSpec((tm, w), lambda i: (i, 0))
    consts = (lw['g1'], lw['wp'], lw['wa2'], lw['ba'], lw['qn'], lw['kn1'], lw['kn2'], lw['gb'], lw['e256'], lw['e128'])
    widths = (512, 256, 256, 256, 128, 256, 256, 512, 256, 128)
    out_shape = [jax.ShapeDtypeStruct((n, w), F32) for w in widths]
    out_specs = [row(w) for w in widths]
    if with_t:
        col = pl.BlockSpec((LANES, tm), lambda i: (0, i))
        out_shape += [jax.ShapeDtypeStruct((n, LANES), BF16), jax.ShapeDtypeStruct((LANES, n), BF16)] * 2
        out_specs += [row(LANES), col] * 2
    return pl.pallas_call(
        _proj_kernel, grid=(n // tm,),
        in_specs=[row(D_MODEL)] + [_const_spec(c.shape) for c in consts],
        out_specs=out_specs, out_shape=out_shape,
        compiler_params=_params(("parallel",)), name="proj",
    )(x, *consts)


_CONV_PAD = 32


def _conv_kernel(a_ref, prev_ref, w_ref, b_ref, lg_ref, lb_ref, y_ref, u_ref, ext_ref, *, tm):
    @pl.when(pl.program_id(1) == 0)
    def _():
        ext_ref[0:_CONV_PAD, :] = prev_ref[0]

    a = a_ref[0]
    u = a[:, :CONV_CH] * jax.nn.sigmoid(a[:, CONV_CH:])
    ext_ref[_CONV_PAD:_CONV_PAD + tm, :] = u
    acc = jnp.zeros((tm, CONV_CH), F32) + b_ref[...]
    first = _CONV_PAD - (CONV_WIDTH - 1)
    for j in range(CONV_WIDTH):
        acc = acc + ext_ref[first + j:first + j + tm, :] * w_ref[j:j + 1, :]
    mu = jnp.mean(acc, axis=-1, keepdims=True)
    cen = acc - mu
    var = jnp.mean(cen * cen, axis=-1, keepdims=True)
    yn = cen * lax.rsqrt(var + EPS) * lg_ref[...] + lb_ref[...]
    y_ref[0] = yn * jax.nn.sigmoid(yn)
    u_ref[0] = u
    ext_ref[0:_CONV_PAD, :] = ext_ref[tm:tm + _CONV_PAD, :]


def _conv_call(a, prev, lw):
    b, t, _ = a.shape
    tm = min(512, t)
    prev = jnp.pad(prev, ((0, 0), (_CONV_PAD - (CONV_WIDTH - 1), 0), (0, 0)))
    blk = lambda w: pl.BlockSpec((1, tm, w), lambda i, j: (i, j, 0))
    consts = (lw['cw'], lw['cb'], lw['clg'], lw['clb'])
    return pl.pallas_call(
        functools.partial(_conv_kernel, tm=tm), grid=(b, t // tm),
        in_specs=[blk(2 * CONV_CH), pl.BlockSpec((1, _CONV_PAD, CONV_CH), lambda i, j: (i, 0, 0))]
        + [_const_spec(c.shape) for c in consts],
        out_specs=[blk(CONV_CH), blk(CONV_CH)],
        out_shape=[jax.ShapeDtypeStruct((b, t, CONV_CH), F32)] * 2,
        scratch_shapes=[pltpu.VMEM((_CONV_PAD + tm, CONV_CH), F32)],
        compiler_params=_params(("arbitrary", "arbitrary")), name="conv",
    )(a, prev, *consts)


_POOL_PAD = 16


def _pool_kernel(u_ref, prev_ref, w_ref, sc_ref, y_ref, ext_ref, *, tm, pos0):
    t = pl.program_id(1)

    @pl.when(t == 0)
    def _():
        ext_ref[0:_POOL_PAD, :] = prev_ref[0]

    u = u_ref[0]
    ext_ref[_POOL_PAD:_POOL_PAD + tm, :] = u

    def back(d):
        return ext_ref[_POOL_PAD - d:_POOL_PAD - d + tm, :]

    sums, acc, d = [], u, 1
    for w in POOL_WINDOWS:
        while d < w:
            acc = acc + back(d)
            d += 1
        sums.append(acc)
    grp = lax.broadcasted_iota(jnp.int32, (tm, POOL_CH), 1) // (POOL_CH // len(POOL_WINDOWS))
    pos = pos0 + t * tm + lax.broadcasted_iota(jnp.int32, (tm, POOL_CH), 0)
    win, width = sums[-1], jnp.full((tm, POOL_CH), POOL_WINDOWS[-1], jnp.int32)
    for gi in range(len(POOL_WINDOWS) - 2, -1, -1):
        win = jnp.where(grp == gi, sums[gi], win)
        width = jnp.where(grp == gi, POOL_WINDOWS[gi], width)
    cnt = jnp.minimum(pos + 1, width).astype(F32)
    pooled = win / cnt - u
    y_ref[0] = _bdot(pooled, w_ref[...]) * sc_ref[...]
    ext_ref[0:_POOL_PAD, :] = ext_ref[tm:tm + _POOL_PAD, :]


def _pool_call(u, prev, pos0, lw):
    b, t, _ = u.shape
    tm = min(512, t)
    prev = jnp.pad(prev, ((0, 0), (_POOL_PAD - POOL_STATE, 0), (0, 0)))
    blk = pl.BlockSpec((1, tm, POOL_CH), lambda i, j: (i, j, 0))
    return pl.pallas_call(
        functools.partial(_pool_kernel, tm=tm, pos0=pos0), grid=(b, t // tm),
        in_specs=[blk, pl.BlockSpec((1, _POOL_PAD, POOL_CH), lambda i, j: (i, 0, 0)),
                  _const_spec(lw['pw'].shape), _const_spec(lw['psc'].shape)],
        out_specs=blk, out_shape=jax.ShapeDtypeStruct((b, t, POOL_CH), F32),
        scratch_shapes=[pltpu.VMEM((_POOL_PAD + tm, POOL_CH), F32)],
        compiler_params=_params(("arbitrary", "arbitrary")), name="pool",
    )(u, prev, lw['pw'], lw['psc'])


def _gla_kernel(qk_ref, v_ref, la_ref, og_ref, s0_ref, on_ref, eh_ref, mbd_ref, e256_ref,
                y_ref, sfin_ref, st_ref, kext, bext, vext, qs_s, ks_s, g_s, o_s, *, tm):
    R = GLA_SUB
    t = pl.program_id(1)
    kw = GLA_HEADS * GLA_DK

    @pl.when(t == 0)
    def _():
        st_ref[...] = s0_ref[0]
        kext[0:R, :] = jnp.zeros((R, kw), F32)
        bext[0:R, :] = jnp.zeros((R, kw), F32)
        vext[0:R, :] = jnp.zeros((R, GLA_HEADS * GLA_DV), F32)

    qk = qk_ref[0]
    q = qk[:, :kw] * (GLA_DK ** -0.5)
    k = qk[:, kw:]
    v = v_ref[0]
    la = la_ref[0]
    r = lax.broadcasted_iota(jnp.int32, (tm, kw), 0) % R
    b = la
    c = la
    for s in (1, 2, 4, 8):
        b = b + jnp.where(r >= s, pltpu.roll(b, s, axis=0), 0.0)
        c = c + jnp.where(r < R - s, pltpu.roll(c, tm - s, axis=0), 0.0)
    kext[R:R + tm, :] = k
    bext[R:R + tm, :] = b
    vext[R:R + tm, :] = v
    o = jnp.zeros((tm, GLA_HEADS * GLA_DV), F32)
    for d in range(R):
        kd = kext[R - d:R - d + tm, :]
        bd = bext[R - d:R - d + tm, :]
        vd = vext[R - d:R - d + tm, :]
        e = jnp.exp(jnp.where(r >= d, b - bd, NEG))
        a = jnp.dot((q * kd * e).astype(BF16), eh_ref[...], preferred_element_type=F32)
        o = o + a * vd
    o_s[...] = o
    qs_s[...] = q * jnp.exp(b)
    ks_s[...] = k * jnp.exp(c - la)
    g_s[...] = jnp.exp(b + c - la)

    nsb = tm // R
    group = 4 if nsb % 4 == 0 else 1

    def body(i, carry):
        st = st_ref[...]
        for u in range(group):
            r0 = pl.multiple_of((i * group + u) * R, R)
            o_s[pl.ds(r0, R), :] += _nt(qs_s[pl.ds(r0, R), :].astype(BF16), st.astype(BF16))
            upd = _tn(vext[pl.ds(R + r0, R), :].astype(BF16), ks_s[pl.ds(r0, R), :].astype(BF16))
            st = st * g_s[pl.ds(r0, 1), :] + upd * mbd_ref[...]
        st_ref[...] = st
        return carry

    lax.fori_loop(0, nsb // group, body, 0)
    o = o_s[...]
    on = o * lax.rsqrt(_segmean(o * o, e256_ref) + EPS) * on_ref[...]
    og = og_ref[0]
    y_ref[0] = on * (og * jax.nn.sigmoid(og))

    @pl.when(t == pl.num_programs(1) - 1)
    def _():
        sfin_ref[0] = st_ref[...]


def _gla_call(qk, v, la, og, s0, lw):
    b, t, _ = qk.shape
    tp = -(-t // GLA_SUB) * GLA_SUB
    if tp != t:
        pad = lambda a: jnp.pad(a, ((0, 0), (0, tp - t), (0, 0)))
        qk, v, la, og = pad(qk), pad(v), pad(la), pad(og)
    tm = min(256, tp)
    kw, vw = GLA_HEADS * GLA_DK, GLA_HEADS * GLA_DV
    eye = jnp.eye(GLA_HEADS, dtype=F32)
    st0 = jnp.einsum('bhkv,hg->bhvgk', s0.astype(F32), eye).reshape(b, vw, kw)
    blk = lambda w: pl.BlockSpec((1, tm, w), lambda i, j: (i, j, 0))
    st_spec = pl.BlockSpec((1, vw, kw), lambda i, j: (i, 0, 0))
    consts = (lw['on'], lw['eh'], lw['mbd'], lw['e256'])
    y, st = pl.pallas_call(
        functools.partial(_gla_kernel, tm=tm), grid=(b, tp // tm),
        in_specs=[blk(2 * kw), blk(vw), blk(kw), blk(vw), st_spec] + [_const_spec(c.shape) for c in consts],
        out_specs=[blk(vw), st_spec],
        out_shape=[jax.ShapeDtypeStruct((b, tp, vw), F32), jax.ShapeDtypeStruct((b, vw, kw), F32)],
        scratch_shapes=[pltpu.VMEM((vw, kw), F32),
                        pltpu.VMEM((GLA_SUB + tm, kw), F32), pltpu.VMEM((GLA_SUB + tm, kw), F32),
                        pltpu.VMEM((GLA_SUB + tm, vw), F32),
                        pltpu.VMEM((tm, kw), F32), pltpu.VMEM((tm, kw), F32), pltpu.VMEM((tm, kw), F32),
                        pltpu.VMEM((tm, vw), F32)],
        compiler_params=_params(("arbitrary", "arbitrary")), name="gla",
    )(qk, v, la, og, st0, *consts)
    st = st.reshape(b, GLA_HEADS, GLA_DV, GLA_HEADS, GLA_DK)
    s_fin = jnp.stack([st[:, h, :, h, :] for h in range(GLA_HEADS)], axis=1).swapaxes(-1, -2)
    return y[:, :t], s_fin


def _cmp_copy(pt_ref, cache_ref, buf, sem, step, slot, j, kind, n_steps, page, feature_major):
    bb, ss = step // n_steps, step % n_steps
    pg = pt_ref[bb, ss * CMP_PAGES + j]
    src = (cache_ref.at[pg, pl.ds(kind * LANES, LANES), :] if feature_major
           else cache_ref.at[pg, :, pl.ds(kind * LANES, LANES)])
    return pltpu.make_async_copy(src, buf.at[slot, kind, pl.ds(j * page, page), :], sem.at[slot])


def _compress_kernel(pt_ref, cache_ref, wk_ref, wv_ref, pek_ref, pev_ref, kn0_ref, e128_ref,
                     kch_ref, kcl_ref, vct_ref, buf, sem, f_s, bias_s, xs, *, n_steps, page, total, feature_major):
    copy = functools.partial(_cmp_copy, pt_ref, cache_ref, buf, sem, n_steps=n_steps, page=page,
                             feature_major=feature_major)
    bi, si = pl.program_id(0), pl.program_id(1)
    step = bi * n_steps + si
    slot = step % 2
    rows = CMP_PAGES * page
    nchunk = rows // CMP_STRIDE

    def start(st, sl):
        for j in range(CMP_PAGES):
            for kind in range(2):
                copy(st, sl, j, kind).start()

    @pl.when(step == 0)
    def _():
        start(step, slot)
        rid = lax.broadcasted_iota(jnp.int32, (CMP_STRIDE, 2 * LANES), 0)
        bk = jnp.zeros((CMP_STRIDE, 2 * LANES), F32)
        bv = jnp.zeros((CMP_STRIDE, 2 * LANES), F32)
        for l in range(CMP_STRIDE):
            pk, pv = pek_ref[...], pev_ref[...]
            w_k = wk_ref[l // 2][(l % 2) * LANES:(l % 2 + 1) * LANES, :]
            w_v = wv_ref[l // 2][(l % 2) * LANES:(l % 2 + 1) * LANES, :]
            tk = jnp.concatenate([_bdot(pk[:, :LANES], w_k[:, :LANES]), _bdot(pk[:, LANES:], w_k[:, LANES:])], axis=1)
            tv = jnp.concatenate([_bdot(pv[:, :LANES], w_v[:, :LANES]), _bdot(pv[:, LANES:], w_v[:, LANES:])], axis=1)
            bk = bk + jnp.where(rid == l, tk, 0.0)
            bv = bv + jnp.where(rid == l, tv, 0.0)
        bias_s[0:1, :] = jnp.sum(bk, axis=0, keepdims=True)
        bias_s[1:2, :] = jnp.sum(bv, axis=0, keepdims=True)

    @pl.when(step + 1 < total)
    def _():
        start(step + 1, 1 - slot)

    for j in range(CMP_PAGES):
        for kind in range(2):
            copy(step, slot, j, kind).wait()
    pitch = xs.shape[1] // CMP_STRIDE
    if feature_major:
        for j in range(CMP_PAGES):
            for kind in range(2):
                x = buf[slot, kind, j * page:(j + 1) * page, :].T
                for v in range(page // SUBLANES):
                    m, l0 = j * (page // CMP_STRIDE) + (v * SUBLANES) // CMP_STRIDE, (v * SUBLANES) % CMP_STRIDE
                    xs[kind, pl.ds(l0 * pitch + m, SUBLANES, stride=pitch), :] = x[v * SUBLANES:(v + 1) * SUBLANES, :]

    def rows_of(kind, l):
        if feature_major:
            return xs[kind, l * pitch:l * pitch + nchunk, :].astype(BF16)
        return buf[slot, kind, pl.ds(l, nchunk, stride=CMP_STRIDE), :].astype(BF16)

    fk = jnp.zeros((nchunk, 2 * LANES), F32)
    fv = jnp.zeros((nchunk, 2 * LANES), F32)
    for l in range(0, CMP_STRIDE, 2):
        xk = jnp.concatenate([rows_of(0, l), rows_of(0, l + 1)], axis=1)
        xv = jnp.concatenate([rows_of(1, l), rows_of(1, l + 1)], axis=1)
        fk = fk + jnp.dot(xk, wk_ref[l // 2], preferred_element_type=F32)
        fv = fv + jnp.dot(xv, wv_ref[l // 2], preferred_element_type=F32)
    m0 = pl.multiple_of(si * nchunk, nchunk)
    f_s[pl.ds(m0, nchunk), 0:2 * LANES] = fk
    f_s[pl.ds(m0, nchunk), 2 * LANES:4 * LANES] = fv

    @pl.when(si == n_steps - 1)
    def _():
        ncp = f_s.shape[0]
        last = lax.broadcasted_iota(jnp.int32, (ncp, LANES), 0) == ncp - 1
        kc = f_s[:, 0:LANES] + pltpu.roll(f_s[:, LANES:2 * LANES], ncp - 1, axis=0)
        kc = kc + bias_s[0:1, 0:LANES] + bias_s[0:1, LANES:2 * LANES]
        kc = jnp.where(last, 0.0, kc)
        kc = kc * lax.rsqrt(_segmean(kc * kc, e128_ref) + EPS) * kn0_ref[...]
        hi, lo = _split(kc)
        kch_ref[0] = hi
        kcl_ref[0] = lo
        vc = f_s[:, 2 * LANES:3 * LANES] + pltpu.roll(f_s[:, 3 * LANES:4 * LANES], ncp - 1, axis=0)
        vc = vc + bias_s[1:2, 0:LANES] + bias_s[1:2, LANES:2 * LANES]
        vct_ref[0] = jnp.where(last, 0.0, vc).T.astype(BF16)


def _compress_call(cache, page_table, lw, feature_major):
    b, n_pages = page_table.shape
    page = cache.shape[2] if feature_major else cache.shape[1]
    assert n_pages % CMP_PAGES == 0 and page % CMP_STRIDE == 0 and (page == LANES or not feature_major)
    n_steps = n_pages // CMP_PAGES
    ncp = n_pages * page // CMP_STRIDE
    rows = CMP_PAGES * page
    consts = (lw['wk'], lw['wv'], lw['pek'], lw['pev'], lw['kn0'], lw['e128'])
    out3 = lambda s: pl.BlockSpec((1,) + s, lambda i, j, pt: (i, 0, 0))
    grid_spec = pltpu.PrefetchScalarGridSpec(
        num_scalar_prefetch=1, grid=(b, n_steps),
        in_specs=[pl.BlockSpec(memory_space=pl.ANY)]
        + [pl.BlockSpec(c.shape, lambda i, j, pt, nd=c.ndim: (0,) * nd) for c in consts],
        out_specs=[out3((ncp, LANES)), out3((ncp, LANES)), out3((LANES, ncp))],
        scratch_shapes=[pltpu.VMEM((2, 2, rows, LANES), F32), pltpu.SemaphoreType.DMA((2,)),
                        pltpu.VMEM((ncp, 4 * LANES), F32), pltpu.VMEM((SUBLANES, 2 * LANES), F32),
                        pltpu.VMEM((2, CMP_STRIDE * (rows // CMP_STRIDE + SUBLANES), LANES), F32)])
    return pl.pallas_call(
        functools.partial(_compress_kernel, n_steps=n_steps, page=page, total=b * n_steps,
                          feature_major=feature_major),
        grid_spec=grid_spec,
        out_shape=[jax.ShapeDtypeStruct((b, ncp, LANES), BF16), jax.ShapeDtypeStruct((b, ncp, LANES), BF16),
                   jax.ShapeDtypeStruct((b, LANES, ncp), BF16)],
        compiler_params=_params(("arbitrary", "arbitrary")), name="compress",
    )(page_table, cache, *consts)


def _softmax_cols(s, mask):
    sm = jnp.where(mask, s, NEG)
    m = jnp.max(sm, axis=0, keepdims=True)
    p = jnp.where(mask, jnp.exp(sm - m), 0.0)
    return p, m, jnp.sum(p, axis=0, keepdims=True)


def _flash_update(carry, s, mask, vt):
    m, l, acc = carry
    sm = jnp.where(mask, s, NEG)
    m_new = jnp.maximum(m, jnp.max(sm, axis=0, keepdims=True))
    alpha = jnp.exp(m - m_new)
    p = jnp.where(mask, jnp.exp(sm - m_new), 0.0)
    l = alpha * l + jnp.sum(p, axis=0, keepdims=True)
    acc = acc * alpha + jnp.dot(vt, p.astype(BF16), preferred_element_type=F32)
    return m_new, l, acc


def _compressed_branch(kch, kcl, vct, qp, colpos):
    qh, ql = _split(qp)
    s = _nt(kch, qh) + _nt(kch, ql) + _nt(kcl, qh)
    n_idx = lax.broadcasted_iota(jnp.int32, s.shape, 0)
    mask = n_idx <= jnp.right_shift(colpos - (2 * CMP_STRIDE - 1), CMP_STRIDE.bit_length() - 1)
    p, _, l = _softmax_cols(s, mask)
    p = p / jnp.maximum(l, 1e-30)
    return p, jnp.dot(vct, p.astype(BF16), preferred_element_type=F32)


def _block_scores(mbt, imp, colpos):
    ih, il = _split(imp)
    blk = jnp.dot(mbt, ih, preferred_element_type=F32) + jnp.dot(mbt, il, preferred_element_type=F32)
    j_idx = lax.broadcasted_iota(jnp.int32, blk.shape, 0)
    avail = j_idx * SEL_BLOCK <= colpos
    forced = (j_idx == colpos // SEL_BLOCK) | (j_idx == 0)
    return jnp.where(avail, jnp.where(forced, FORCE, blk), NEG)


def _pick_round(x, j_f):
    m = jnp.max(x, axis=0, keepdims=True)
    first = jnp.min(jnp.where(x == m, j_f, 1e9), axis=0, keepdims=True)
    pick = j_f == first
    return m, first, pick, jnp.where(pick, -jnp.inf, x)


def _head_rows_to_lanes(o_rows, gates, rows_per_head):
    n = rows_per_head
    lo_half = lax.broadcasted_iota(jnp.int32, (n, LANES), 1) < NSA_HD
    outs = []
    for h in range(NSA_HEADS):
        acc = None
        for c, o in enumerate(o_rows):
            term = gates[:, 3 * h + c:3 * h + c + 1] * o[h * n:(h + 1) * n, :]
            acc = term if acc is None else acc + term
        outs.append(acc)
    y01 = jnp.where(lo_half, outs[0], pltpu.roll(outs[1], NSA_HD, axis=1))
    y23 = jnp.where(lo_half, pltpu.roll(outs[2], NSA_HD, axis=1), outs[3])
    return jnp.concatenate([y01, y23], axis=1)


def _nsa_prompt_kernel(q_ref, ng_ref, kch_ref, kcl_ref, vct_ref, mbt_ref, ks_ref, svt_ref, kw_ref, wvt_ref,
                       e16_ref, y_ref, sel_s, oc_s, *, n_variants):
    qb_rows = Q_BLOCK
    s0 = pl.program_id(1) * qb_rows
    q = q_ref[0]
    lo_half = lax.broadcasted_iota(jnp.int32, (qb_rows, LANES), 1) < NSA_HD
    q01, q23 = q[:, :LANES], q[:, LANES:]
    qp = jnp.concatenate([jnp.where(lo_half, q01, 0.0),
                          jnp.where(lo_half, pltpu.roll(q01, NSA_HD, axis=1), 0.0),
                          jnp.where(lo_half, 0.0, pltpu.roll(q23, NSA_HD, axis=1)),
                          jnp.where(lo_half, 0.0, q23)], axis=0)
    ncols = NSA_HEADS * qb_rows
    colpos = s0 + lax.broadcasted_iota(jnp.int32, (1, ncols), 1) % qb_rows
    qb = qp.astype(BF16)

    ncp, nbp = kch_ref.shape[1], sel_s.shape[0]

    def select(nc, nb):
        p, oc_t = _compressed_branch(kch_ref[0, :nc, :], kcl_ref[0, :nc, :], vct_ref[0, :, :nc], qp, colpos)
        oc_s[...] = oc_t
        imp = jnp.concatenate([p[:, 0:qb_rows] + p[:, qb_rows:2 * qb_rows],
                               p[:, 2 * qb_rows:3 * qb_rows] + p[:, 3 * qb_rows:]], axis=1)
        x0 = _block_scores(mbt_ref[:nb, :nc], imp, colpos[:, :2 * qb_rows])
        j_f = lax.broadcasted_iota(jnp.int32, x0.shape, 0).astype(F32)
        x = x0
        for _ in range(N_SEL):
            _, _, _, x = _pick_round(x, j_f)
        sel = jnp.where((x == -jnp.inf) & (x0 > 0.5 * NEG), 0.0, NEG)
        sel_s[0:nb, :] = jnp.concatenate([sel[:, :qb_rows], sel[:, :qb_rows], sel[:, qb_rows:], sel[:, qb_rows:]], axis=1)

    variant = pl.program_id(1) // (pl.num_programs(1) // n_variants)
    for v in range(n_variants):
        pl.when(variant == v)(functools.partial(select, ncp * (v + 1) // n_variants, nbp * (v + 1) // n_variants))
    oc_t = oc_s[...]

    kt = SEL_TILE
    nblk = kt // SEL_BLOCK
    qpt = qp.T.astype(BF16)
    pad_rows = jnp.zeros((LANES - nblk, ncols), BF16)

    def scores(c):
        k0 = pl.multiple_of(c * kt, kt)
        keys = jnp.concatenate([ks_ref[0, pl.ds(k0, kt), :], e16_ref[...]], axis=1)
        bias = sel_s[pl.ds(pl.multiple_of(c * nblk, nblk), nblk), :].astype(BF16)
        return k0, jnp.dot(keys, jnp.concatenate([qpt, bias, pad_rows], axis=0), preferred_element_type=F32)

    def unmasked(carry, k0, s):
        m, l, acc = carry
        m_new = jnp.maximum(m, jnp.max(s, axis=0, keepdims=True))
        alpha = jnp.exp(m - m_new)
        p = jnp.exp(s - m_new)
        l = alpha * l + jnp.sum(p, axis=0, keepdims=True)
        acc = acc * alpha + jnp.dot(svt_ref[0, :, pl.ds(k0, kt)], p.astype(BF16), preferred_element_type=F32)
        return m_new, l, acc

    def pair(c, carry):
        ka, sa = scores(2 * c)
        kb, sb = scores(2 * c + 1)
        return unmasked(unmasked(carry, ka, sa), kb, sb)

    def masked(c, carry):
        k0, s = scores(c)
        return _flash_update(carry, s, lax.broadcasted_iota(jnp.int32, s.shape, 0) <= colpos - k0,
                             svt_ref[0, :, pl.ds(k0, kt)])

    init = (jnp.full((1, ncols), NEG, F32), jnp.zeros((1, ncols), F32), jnp.zeros((LANES, ncols), F32))
    last = (s0 + qb_rows - 1) // kt
    carry = lax.fori_loop(0, last // 2, pair, init)
    _, l, acc = lax.fori_loop(2 * (last // 2), last + 1, masked, carry)
    os_t = acc / jnp.maximum(l, 1e-30)

    span = WINDOW + qb_rows
    w0 = pl.multiple_of(jnp.maximum(s0 - WINDOW, 0), qb_rows)
    s = _nt(kw_ref[0, pl.ds(w0, span), :], qb)
    krow = lax.broadcasted_iota(jnp.int32, s.shape, 0)
    newest = colpos - w0
    pw, _, lw_ = _softmax_cols(s, (krow <= newest) & (krow > newest - WINDOW))
    ow_t = jnp.dot(wvt_ref[0, :, pl.ds(w0, span)], pw.astype(BF16), preferred_element_type=F32) / jnp.maximum(lw_, 1e-30)

    y_ref[0] = _head_rows_to_lanes([oc_t.T, os_t.T, ow_t.T], ng_ref[0], qb_rows)


def _nsa_prompt_call(q, ng, kch, kcl, vct, ksb, svt, kwb, wvt, lw):
    b, t, _ = q.shape
    assert t % SEL_TILE == 0 and t >= WINDOW + Q_BLOCK
    ncp = kch.shape[1]
    nbp = t // SEL_BLOCK
    n_variants = max(v for v in (1, 2, 4) if (t // v) % SEL_TILE == 0 and (ncp // v) % LANES == 0)
    full = lambda s: pl.BlockSpec((1,) + s, lambda i, j: (i, 0, 0))
    blk = lambda w: pl.BlockSpec((1, Q_BLOCK, w), lambda i, j: (i, j, 0))
    return pl.pallas_call(
        functools.partial(_nsa_prompt_kernel, n_variants=n_variants), grid=(b, t // Q_BLOCK),
        in_specs=[blk(2 * LANES), blk(LANES), full((ncp, LANES)), full((ncp, LANES)), full((LANES, ncp)),
                  _const_spec(lw['mbt_p'].shape), full((t, LANES)), full((LANES, t)), full((t, LANES)),
                  full((LANES, t)), _const_spec(lw['e16'].shape)],
        out_specs=blk(2 * LANES), out_shape=jax.ShapeDtypeStruct((b, t, 2 * LANES), F32),
        scratch_shapes=[pltpu.VMEM((nbp, NSA_HEADS * Q_BLOCK), F32), pltpu.VMEM((LANES, NSA_HEADS * Q_BLOCK), F32)],
        compiler_params=_params(("arbitrary", "arbitrary")), name="nsa_prompt",
    )(q, ng, kch, kcl, vct, lw['mbt_p'], ksb, svt, kwb, wvt, lw['e16'])


_SCOL = SUBLANES


def _nsa_sample_a_kernel(qp_ref, kch_ref, kcl_ref, vct_ref, mbt_ref, cwin_ref, wnew_ref,
                         oc_ref, ow_ref, idx_ref, ok_ref, *, past_len, n_dec):
    qp = qp_ref[0]
    qb = qp.astype(BF16)
    col = lax.broadcasted_iota(jnp.int32, (1, LANES), 1)
    tcol = col % _SCOL
    colpos = past_len + tcol
    p, oc_t = _compressed_branch(kch_ref[0], kcl_ref[0], vct_ref[0], qp, colpos)
    oc_ref[0] = oc_t
    first_head = (col % (2 * _SCOL)) < _SCOL
    imp = p + jnp.where(first_head, pltpu.roll(p, LANES - _SCOL, axis=1), pltpu.roll(p, _SCOL, axis=1))
    x = _block_scores(mbt_ref[...], imp, colpos)
    j_f = lax.broadcasted_iota(jnp.int32, x.shape, 0).astype(F32)
    for rd in range(N_SEL - 1):
        m, first, _, x = _pick_round(x, j_f)
        idx_ref[0, rd:rd + 1, :] = first.astype(jnp.int32)
        ok_ref[0, rd:rd + 1, :] = jnp.where(m > 0.5 * NEG, 1.0, 0.0)
    idx_ref[0, N_SEL - 1:N_SEL, :] = jnp.zeros((1, LANES), jnp.int32)
    ok_ref[0, N_SEL - 1:N_SEL, :] = jnp.ones((1, LANES), F32)

    kp, kn = cwin_ref[0], wnew_ref[0]
    wbuf = kp.shape[0]
    s1 = _nt(kp[:, :LANES].astype(BF16), qb)
    kpos1 = past_len - wbuf + lax.broadcasted_iota(jnp.int32, s1.shape, 0)
    d1 = colpos - kpos1
    mask1 = (d1 >= 0) & (d1 < WINDOW) & (kpos1 >= 0)
    s2 = _nt(kn[:, :LANES].astype(BF16), qb)
    kidx = lax.broadcasted_iota(jnp.int32, s2.shape, 0)
    mask2 = (kidx <= tcol) & (kidx < n_dec)
    carry = (jnp.full((1, LANES), NEG, F32), jnp.zeros((1, LANES), F32), jnp.zeros((LANES, LANES), F32))
    carry = _flash_update(carry, s1, mask1, kp[:, LANES:].T.astype(BF16))
    _, l, acc = _flash_update(carry, s2, mask2, kn[:, LANES:].T.astype(BF16))
    ow_ref[0] = acc / jnp.maximum(l, 1e-30)


def _nsa_sample_a_call(qp, kch, kcl, vct, cwin, wnew, lw, past_len, n_dec):
    b = qp.shape[0]
    ncp = kch.shape[1]
    full = lambda a: pl.BlockSpec((1,) + a.shape[1:], lambda i: (i, 0, 0))
    sq = pl.BlockSpec((1, LANES, LANES), lambda i: (i, 0, 0))
    rnd = pl.BlockSpec((1, N_SEL, LANES), lambda i: (i, 0, 0))
    return pl.pallas_call(
        functools.partial(_nsa_sample_a_kernel, past_len=past_len, n_dec=n_dec), grid=(b,),
        in_specs=[sq, full(kch), full(kcl), full(vct), _const_spec(lw['mbt_s'].shape), full(cwin), full(wnew)],
        out_specs=[sq, sq, rnd, rnd],
        out_shape=[jax.ShapeDtypeStruct((b, LANES, LANES), F32)] * 2
        + [jax.ShapeDtypeStruct((b, N_SEL, LANES), jnp.int32), jax.ShapeDtypeStruct((b, N_SEL, LANES), F32)],
        compiler_params=_params(("parallel",)), name="nsa_sample_a",
    )(qp, kch, kcl, vct, lw['mbt_s'], cwin, wnew)


def _sel_copy(pt_ref, info_ref, cache_ref, bufs, sem, bi, slot, c, rd, kind, n_dec, page):
    j = info_ref[bi, c * (N_SEL - 1) + rd]
    pg = pt_ref[bi, j // (page // SEL_BLOCK)]
    feat = (2 + kind) * LANES + (c // n_dec) * NSA_HD
    return pltpu.make_async_copy(cache_ref.at[pg, pl.ds(feat, NSA_HD), :],
                                 bufs[kind].at[slot, c, :, pl.ds(rd * page, page)], sem.at[slot])


def _nsa_sample_b_kernel(pt_ref, info_ref, cache_ref, qpt_ref, newt_ref, oc_ref, ow_ref, ngt_ref,
                         y_ref, kbuf, vbuf, sem, *, n_dec, page):
    bi = pl.program_id(0)
    slot = bi % 2
    ncomb = NSA_KV * n_dec
    nsel = N_SEL - 1
    past_keys = nsel * page
    copy = functools.partial(_sel_copy, pt_ref, info_ref, cache_ref, (kbuf, vbuf), sem, n_dec=n_dec, page=page)

    def start(b, sl):
        for c in range(ncomb):
            for rd in range(nsel):
                for kind in range(2):
                    copy(b, sl, c, rd, kind).start()

    @pl.when(bi == 0)
    def _():
        start(bi, slot)

    @pl.when(bi + 1 < pl.num_programs(0))
    def _():
        start(bi + 1, 1 - slot)

    newt = newt_ref[0]
    for c in range(ncomb):
        g = c // n_dec
        kbuf[slot, c, :, past_keys:past_keys + LANES] = newt[g * NSA_HD:(g + 1) * NSA_HD, :]
        vbuf[slot, c, :, past_keys:past_keys + LANES] = newt[LANES + g * NSA_HD:LANES + (g + 1) * NSA_HD, :]
    for c in range(ncomb):
        for rd in range(nsel):
            for kind in range(2):
                copy(bi, slot, c, rd, kind).wait()

    qpt = qpt_ref[0]
    lane = lax.broadcasted_iota(jnp.int32, (1, LANES), 1)
    lane_p = lax.broadcasted_iota(jnp.int32, (1, page), 1)
    rep = NSA_HEADS // NSA_KV
    os_h = [jnp.zeros((NSA_HD, LANES), F32) for _ in range(NSA_HEADS)]
    for c in range(ncomb):
        g, t = divmod(c, n_dec)
        pieces = []
        for rd in range(nsel):
            j = info_ref[bi, c * nsel + rd]
            ok = info_ref[bi, ncomb * nsel + c * nsel + rd]
            chosen = (lane_p // SEL_BLOCK == j % (page // SEL_BLOCK)) & (ok > 0)
            pieces.append(jnp.where(chosen, 0.0, NEG))
        pieces.append(jnp.where(lane <= t, 0.0, NEG))
        bias = jnp.concatenate(pieces, axis=1)
        keys, vals = kbuf[slot, c], vbuf[slot, c]
        for r in range(rep):
            h = g * rep + r
            cidx = h * _SCOL + t
            qcol = qpt[g * NSA_HD:(g + 1) * NSA_HD, cidx:cidx + 1]
            s = jnp.sum(keys * qcol, axis=0, keepdims=True) + bias
            m = jnp.max(s, axis=1, keepdims=True)
            p = jnp.exp(s - m)
            l = jnp.sum(p, axis=1, keepdims=True)
            o = jnp.sum(vals * p, axis=1, keepdims=True) / jnp.maximum(l, 1e-30)
            os_h[h] = jnp.where(lane == t, o, os_h[h])
    oc, ow, ngt = oc_ref[0], ow_ref[0], ngt_ref[0]
    for h in range(NSA_HEADS):
        g = h // rep

        def to_front(a):
            blk = a[g * NSA_HD:(g + 1) * NSA_HD, :]
            return blk if h == 0 else pltpu.roll(blk, LANES - h * _SCOL, axis=1)

        y_ref[0, h * NSA_HD:(h + 1) * NSA_HD, :] = (ngt[3 * h:3 * h + 1, :] * to_front(oc)
                                                   + ngt[3 * h + 1:3 * h + 2, :] * os_h[h]
                                                   + ngt[3 * h + 2:3 * h + 3, :] * to_front(ow))


def _nsa_sample_b_call(page_table, info, cache_t, qpt, newt, oc, ow, ngt, n_dec):
    b = qpt.shape[0]
    page = cache_t.shape[2]
    ncomb = NSA_KV * n_dec
    keys = (N_SEL - 1) * page + LANES
    spec = lambda a: pl.BlockSpec((1,) + a.shape[1:], lambda i, pt, sl: (i, 0, 0))
    grid_spec = pltpu.PrefetchScalarGridSpec(
        num_scalar_prefetch=2, grid=(b,),
        in_specs=[pl.BlockSpec(memory_space=pl.ANY), spec(qpt), spec(newt), spec(oc), spec(ow), spec(ngt)],
        out_specs=pl.BlockSpec((1, NSA_HEADS * NSA_HD, LANES), lambda i, pt, sl: (i, 0, 0)),
        scratch_shapes=[pltpu.VMEM((2, ncomb, NSA_HD, keys), F32), pltpu.VMEM((2, ncomb, NSA_HD, keys), F32),
                        pltpu.SemaphoreType.DMA((2,))])
    return pl.pallas_call(
        functools.partial(_nsa_sample_b_kernel, n_dec=n_dec, page=page), grid_spec=grid_spec,
        out_shape=jax.ShapeDtypeStruct((b, NSA_HEADS * NSA_HD, LANES), F32),
        compiler_params=_params(("arbitrary",)), name="nsa_sample_b",
    )(page_table, info, cache_t, qpt, newt, oc, ow, ngt)


def _merge_kernel(x_ref, g1_ref, wg_ref, ya_ref, yb_ref, yc_ref, yd_ref, wout_ref, wo_ref, o_ref):
    x = x_ref[...]
    h = _rms(x, g1_ref[...]).astype(BF16)
    mix = None
    for i, y_ref in enumerate((ya_ref, yb_ref, yc_ref, yd_ref)):
        gate = jax.nn.sigmoid(_nt(h, wg_ref[i * D_MODEL:(i + 1) * D_MODEL, :]))
        term = gate * _bdot(y_ref[...], wout_ref[i])
        mix = term if mix is None else mix + term
    o_ref[...] = x + _bdot(mix, wo_ref[...])


def _merge_call(x, ys, lw):
    n = x.shape[0]
    tm = min(512, n)
    row = lambda w: pl.BlockSpec((tm, w), lambda i: (i, 0))
    return pl.pallas_call(
        _merge_kernel, grid=(n // tm,),
        in_specs=[row(D_MODEL), _const_spec(lw['g1'].shape), _const_spec(lw['wgate'].shape)] + [row(2 * LANES)] * 4
        + [_const_spec(lw['wout'].shape), _const_spec(lw['wo'].shape)],
        out_specs=row(D_MODEL), out_shape=jax.ShapeDtypeStruct((n, D_MODEL), F32),
        compiler_params=_params(("parallel",)), name="merge",
    )(x, lw['g1'], lw['wgate'], *ys, lw['wout'], lw['wo'])


_FFN_CHUNK = 256


def _ffn_kernel(x_ref, g2_ref, wg_ref, wu_ref, wd_ref, o_ref):
    x = x_ref[...]
    h = _rms(x, g2_ref[...]).astype(BF16)
    acc = x
    for c in range(0, D_FF, _FFN_CHUNK):
        g = jnp.dot(h, wg_ref[:, c:c + _FFN_CHUNK], preferred_element_type=F32)
        u = jnp.dot(h, wu_ref[:, c:c + _FFN_CHUNK], preferred_element_type=F32)
        acc = acc + _bdot(g * jax.nn.sigmoid(g) * u, wd_ref[c:c + _FFN_CHUNK, :])
    o_ref[...] = acc


def _ffn_call(x, lw):
    n = x.shape[0]
    tm = min(512, n)
    row = pl.BlockSpec((tm, D_MODEL), lambda i: (i, 0))
    return pl.pallas_call(
        _ffn_kernel, grid=(n // tm,),
        in_specs=[row, _const_spec(lw['g2'].shape), _const_spec(lw['fg'].shape), _const_spec(lw['fu'].shape),
                  _const_spec(lw['fd'].shape)],
        out_specs=row, out_shape=jax.ShapeDtypeStruct((n, D_MODEL), F32),
        compiler_params=_params(("parallel",)), name="ffn",
    )(x, lw['g2'], lw['fg'], lw['fu'], lw['fd'])


def _block_diag(blocks):
    g, a, b = blocks.shape[-3:]
    eye = jnp.eye(g, dtype=blocks.dtype)
    out = jnp.einsum('...gab,gh->...gahb', blocks, eye)
    return out.reshape(blocks.shape[:-3] + (g * a, g * b))


def _seg_mean_matrix(width):
    seg = np.arange(width) // NSA_HD
    return jnp.asarray(np.where(seg[:, None] == seg[None, :], 1.0 / NSA_HD, 0.0), BF16)


def _band_matrix(nbp, ncp):
    ratio = SEL_BLOCK // CMP_STRIDE
    j = np.arange(nbp)[:, None]
    n = np.arange(ncp)[None, :]
    return jnp.asarray((n >= ratio * j - 1) & (n <= ratio * j + ratio - 1), BF16)


def _expand_matrix(nrows, nblk):
    return jnp.asarray(np.arange(nrows)[:, None] // SEL_BLOCK == np.arange(nblk)[None, :], BF16)


def _prep_layer(w, l, t_prompt, past_len):
    row = lambda a: a.reshape(1, -1).astype(F32)
    w_in_t = jnp.transpose(w['w_in'], (2, 0, 1))[:, l, :]
    pad_to = lambda a, n: jnp.pad(a, ((0, n - a.shape[0]), (0, 0)))
    wp = jnp.concatenate([w_in_t[0:1280], pad_to(w_in_t[1280:1296], LANES), w_in_t[1296:2576],
                          pad_to(w_in_t[2576:2588], LANES)], axis=0).astype(BF16)
    phi, pe = w['nsa_phi'][l], w['nsa_pe'][l]
    half = CMP_STRIDE

    def cmp_w(p):
        bd = lambda a: _block_diag(jnp.broadcast_to(a[:, None], (half, NSA_KV) + a.shape[1:]))
        per_l = jnp.concatenate([bd(p[:half]), bd(p[half:])], axis=-1)
        return per_l.reshape(half // 2, 2 * LANES, 2 * LANES).astype(BF16)

    def cmp_pe(p):
        return jnp.concatenate([jnp.tile(p[:half], (1, NSA_KV)), jnp.tile(p[half:], (1, NSA_KV))], axis=1).astype(F32)

    kw = GLA_HEADS * GLA_DK
    hk = np.arange(kw) // GLA_DK
    hv = np.arange(GLA_HEADS * GLA_DV) // GLA_DV
    same_head = hk[:, None] == hv[None, :]
    return {
        'g1': row(w['norm1'][l]), 'g2': row(w['norm2'][l]), 'wp': wp, 'wgate': w_in_t[2588:].astype(BF16),
        'wa2': jnp.pad(w['gla_wa2'][l], ((0, LANES - GLA_LOWRANK), (0, 0))).astype(BF16), 'ba': row(w['gla_ba'][l]),
        'qn': row(jnp.tile(w['nsa_qn'][l], NSA_HEADS)) * (NSA_HD ** -0.5),
        'kn0': row(jnp.tile(w['nsa_kn'][l][0], NSA_KV)), 'kn1': row(jnp.tile(w['nsa_kn'][l][1], NSA_KV)),
        'kn2': row(jnp.tile(w['nsa_kn'][l][2], NSA_KV)),
        'gb': row(jnp.pad(w['nsa_gb'][l], (0, LANES - 3 * NSA_HEADS))),
        'e256': _seg_mean_matrix(2 * LANES), 'e128': _seg_mean_matrix(LANES),
        'cw': jnp.pad(w['conv_w'][l], ((0, 1), (0, 0))).astype(F32), 'cb': row(w['conv_b'][l]),
        'clg': row(w['conv_ln_g'][l]), 'clb': row(w['conv_ln_b'][l]),
        'pw': _block_diag(w['pool_w'][l]).astype(BF16), 'psc': row(w['pool_scale'][l]),
        'on': row(jnp.tile(w['gla_onorm'][l], GLA_HEADS)),
        'eh': jnp.asarray(same_head, BF16), 'mbd': jnp.asarray(same_head.T, F32),
        'wk': cmp_w(phi[0]), 'wv': cmp_w(phi[1]), 'pek': cmp_pe(pe[0]), 'pev': cmp_pe(pe[1]),
        'mbt_p': _band_matrix(t_prompt // SEL_BLOCK, t_prompt // CMP_STRIDE),
        'mbt_s': _band_matrix(past_len // SEL_BLOCK, past_len // CMP_STRIDE),
        'e16': _expand_matrix(SEL_TILE, LANES),
        'wout': jnp.stack([w['w_out_conv'][l], w['w_out_gla'][l], w['w_out_pool'][l], w['w_out_nsa'][l]]).astype(BF16),
        'wo': w['w_o'][l].astype(BF16),
        'fg': w['ffn_gate'][l].astype(BF16), 'fu': w['ffn_up'][l].astype(BF16), 'fd': w['ffn_down'][l].astype(BF16),
    }


def _prompt_layer(x, lw):
    b, t, _ = x.shape
    xf = x.reshape(b * t, D_MODEL)
    a, qk, v, og, la, up, q, rows, win, ng, ksb, svt, kwb, wvt = _proj_call(xf, lw, True)
    r3 = lambda z: z.reshape(b, t, z.shape[-1])
    ya, u = _conv_call(r3(a), jnp.zeros((b, CONV_WIDTH - 1, CONV_CH), F32), lw)
    yc = _pool_call(r3(up), jnp.zeros((b, POOL_STATE, POOL_CH), F32), 0, lw)
    yb, gla_state = _gla_call(r3(qk), r3(v), r3(la), r3(og), jnp.zeros((b, GLA_HEADS, GLA_DK, GLA_DV), F32), lw)
    page = 2 * SEL_BLOCK
    cache = rows.reshape(b * t // page, page, rows.shape[-1])
    table = jnp.arange(b * t // page, dtype=jnp.int32).reshape(b, t // page)
    kch, kcl, vct = _compress_call(cache, table, lw, False)
    tcols =lambda z: z.reshape(LANES, b, t).swapaxes(0, 1)
    yd = _nsa_prompt_call(r3(q), r3(ng), kch, kcl, vct, r3(ksb), tcols(svt), r3(kwb), tcols(wvt), lw)
    x1 = _merge_call(xf, [ya.reshape(b * t, -1), yb.reshape(b * t, -1), yc.reshape(b * t, -1), yd.reshape(b * t, -1)], lw)
    x2 = _ffn_call(x1, lw).reshape(b, t, D_MODEL)
    keep = min(WINDOW, t)
    return (x2, r3(rows).reshape(b, t, 4, NSA_KV, NSA_HD), r3(win)[:, -keep:].reshape(b, keep, 2, NSA_KV, NSA_HD),
            u[:, -(CONV_WIDTH - 1):], r3(up)[:, -POOL_STATE:], gla_state)


def _sample_layer(x, lw, cache_t, cwin, conv_prev, pool_prev, gla_prev, page_table):
    b, t, _ = x.shape
    page = cache_t.shape[2]
    past_len = page_table.shape[1] * page
    assert past_len % SEL_BLOCK == 0 and t <= _SCOL and page % SEL_BLOCK == 0
    xf = x.reshape(b * t, D_MODEL)
    a, qk, v, og, la, up, q, rows, win, ng = _proj_call(xf, lw, False)
    r3 = lambda z: z.reshape(b, t, z.shape[-1])
    ya, u = _conv_call(r3(a), conv_prev, lw)
    yc = _pool_call(r3(up), pool_prev, past_len, lw)
    yb, gla_state = _gla_call(r3(qk), r3(v), r3(la), r3(og), gla_prev, lw)
    kch, kcl, vct = _compress_call(cache_t, page_table, lw, True)
    q4 = jnp.pad(r3(q).reshape(b, t, NSA_HEADS, NSA_HD), ((0, 0), (0, _SCOL - t), (0, 0), (0, 0))).swapaxes(1, 2)
    grp = jnp.arange(NSA_HEADS) // (NSA_HEADS // NSA_KV)
    onehot = (grp[:, None] == jnp.arange(NSA_KV)[None, :]).astype(F32)
    qp = jnp.einsum('bhtd,hg->bhtgd', q4, onehot).reshape(b, NSA_HEADS * _SCOL, LANES)
    qp = jnp.pad(qp, ((0, 0), (0, LANES - NSA_HEADS * _SCOL), (0, 0)))
    pad_rows = lambda z, n: jnp.pad(z, ((0, 0), (0, n - z.shape[1]), (0, 0)))
    cwin2 = cwin.reshape(b, cwin.shape[1], -1)
    oc, ow, idx, ok = _nsa_sample_a_call(qp, kch, kcl, vct, cwin2, pad_rows(r3(win), LANES), lw, past_len, t)
    cols = (jnp.arange(NSA_KV)[:, None] * (NSA_HEADS // NSA_KV) * _SCOL + jnp.arange(t)[None, :]).reshape(-1)
    per_comb = lambda z: z[:, :N_SEL - 1, :][:, :, cols].swapaxes(1, 2).reshape(b, -1)
    info = jnp.concatenate([per_comb(idx), per_comb(ok).astype(jnp.int32)], axis=1)
    lanes_t = lambda z, n: jnp.pad(z.swapaxes(1, 2), ((0, 0), (0, n - z.shape[2]), (0, LANES - z.shape[1])))
    yt = _nsa_sample_b_call(page_table, info, cache_t, qp.swapaxes(1, 2), lanes_t(r3(rows)[:, :, 2 * LANES:], 2 * LANES),
                            oc, ow, lanes_t(r3(ng)[:, :, :3 * NSA_HEADS], 2 * SUBLANES), t)
    yd = yt[:, :, :t].swapaxes(1, 2)
    x1 = _merge_call(xf, [ya.reshape(b * t, -1), yb.reshape(b * t, -1), yc.reshape(b * t, -1), yd.reshape(b * t, -1)], lw)
    x2 = _ffn_call(x1, lw).reshape(b, t, D_MODEL)
    win_all = jnp.concatenate([cwin2, r3(win)], axis=1)
    keep = min(WINDOW, win_all.shape[1])
    return (x2, r3(rows).reshape(b, t, 4, NSA_KV, NSA_HD), win_all[:, -keep:].reshape(b, keep, 2, NSA_KV, NSA_HD),
            jnp.concatenate([conv_prev, u], axis=1)[:, -(CONV_WIDTH - 1):],
            jnp.concatenate([pool_prev, r3(up)], axis=1)[:, -POOL_STATE:], gla_state)


def kernel(x_prompt, x_sample, cache_nsa_kv, cache_win_kv, state_conv, state_pool, state_gla, page_table, norm1, w_in, conv_w, conv_b, conv_ln_g, conv_ln_b, w_out_conv, gla_wa2, gla_ba, gla_onorm, w_out_gla, pool_w, pool_scale, w_out_pool, nsa_qn, nsa_kn, nsa_pe, nsa_phi, nsa_gb, w_out_nsa, w_o, norm2, ffn_gate, ffn_up, ffn_down):
    w = dict(norm1=norm1, w_in=w_in, conv_w=conv_w, conv_b=conv_b, conv_ln_g=conv_ln_g, conv_ln_b=conv_ln_b,
             w_out_conv=w_out_conv, gla_wa2=gla_wa2, gla_ba=gla_ba, gla_onorm=gla_onorm, w_out_gla=w_out_gla,
             pool_w=pool_w, pool_scale=pool_scale, w_out_pool=w_out_pool, nsa_qn=nsa_qn, nsa_kn=nsa_kn,
             nsa_pe=nsa_pe, nsa_phi=nsa_phi, nsa_gb=nsa_gb, w_out_nsa=w_out_nsa, w_o=w_o, norm2=norm2,
             ffn_gate=ffn_gate, ffn_up=ffn_up, ffn_down=ffn_down)
    depth, n_pool, page = cache_nsa_kv.shape[:3]
    past_len = page_table.shape[1] * page
    cache = jnp.moveaxis(cache_nsa_kv, 2, -1).reshape(depth * n_pool, -1, page)
    xp, xs = x_prompt, x_sample
    outs_p, outs_s = [], []
    for l in range(depth):
        lw = _prep_layer(w, l, xp.shape[1], past_len)
        xp, *st_p = _prompt_layer(xp, lw)
        xs, *st_s = _sample_layer(xs, lw, cache, cache_win_kv[l], state_conv[l], state_pool[l],
                                  state_gla[l], page_table + l * n_pool)
        outs_p.append(st_p)
        outs_s.append(st_s)
    stack = lambda outs, i: jnp.stack([o[i] for o in outs])
    return (xp, xs) + tuple(stack(outs_p, i) for i in range(5)) + tuple(stack(outs_s, i) for i in range(5))
```

```python
import functools

import numpy as np
import jax
import jax.numpy as jnp
from jax import lax
from jax.experimental import pallas as pl
from jax.experimental.pallas import tpu as pltpu

F32, BF16 = jnp.float32, jnp.bfloat16

D_MODEL = 1024
CONV_CH = 256
CONV_WIDTH = 31
GLA_HEADS, GLA_DK, GLA_DV = 4, 32, 64
GLA_LOWRANK = 16
GLA_TAU = 16.0
POOL_CH = 256
POOL_WINDOWS = (2, 4, 8, 16)
POOL_STATE = 15
NSA_HEADS, NSA_HD, NSA_KV = 4, 64, 2
CMP_STRIDE = 16
SEL_BLOCK = 64
N_SEL = 16
WINDOW = 512
Q_BLOCK = 128
D_FF = 2816
EPS = 1e-6
NEG = -1e30
FORCE = 1e4

LANES = 128
SUBLANES = 8
VMEM_LIMIT = 56 * 1024 * 1024

GLA_SUB = 16
SEL_TILE = 1024
CMP_PAGES = 16

_P_A, _P_QK, _P_V, _P_O, _P_LR, _P_UP, _P_NQ, _P_KV, _P_NG, _P_END = (
    0, 512, 768, 1024, 1280, 1408, 1664, 1920, 2688, 2816)


def _bdot(a, b):
    return jnp.dot(a.astype(BF16), b.astype(BF16), preferred_element_type=F32)


def _nt(a, b):
    return lax.dot_general(a, b, (((1,), (1,)), ((), ())), preferred_element_type=F32)


def _tn(a, b):
    return lax.dot_general(a, b, (((0,), (0,)), ((), ())), preferred_element_type=F32)


def _split(x):
    hi = x.astype(BF16)
    return hi, (x - hi.astype(F32)).astype(BF16)


def _rms(x, g):
    return x * lax.rsqrt(jnp.mean(x * x, axis=-1, keepdims=True) + EPS) * g


def _segmean(x2, e_ref):
    hi, lo = _split(x2)
    e = e_ref[...]
    return jnp.dot(hi, e, preferred_element_type=F32) + jnp.dot(lo, e, preferred_element_type=F32)


def _const_spec(shape):
    nd = len(shape)
    return pl.BlockSpec(shape, lambda *_: (0,) * nd)


def _params(sem):
    return pltpu.CompilerParams(dimension_semantics=sem, vmem_limit_bytes=VMEM_LIMIT)


def _proj_kernel(x_ref, g1_ref, wp_ref, wa2_ref, ba_ref, qn_ref, kn1_ref, kn2_ref, gb_ref, e256_ref, e128_ref,
                 a_ref, qk_ref, v_ref, o_ref, la_ref, up_ref, q_ref, rows_ref, win_ref, ng_ref, *t_refs):
    h = _rms(x_ref[...], g1_ref[...]).astype(BF16)

    def seg(a, b):
        return _nt(h, wp_ref[a:b, :])

    a_ref[...] = seg(_P_A, _P_QK)
    qk_ref[...] = seg(_P_QK, _P_V)
    v_ref[...] = seg(_P_V, _P_O)
    o_ref[...] = seg(_P_O, _P_LR)
    z = _bdot(seg(_P_LR, _P_UP), wa2_ref[...]) + ba_ref[...]
    la_ref[...] = (jnp.minimum(z, 0.0) - jnp.log(1.0 + jnp.exp(-jnp.abs(z)))) * (1.0 / GLA_TAU)
    up_ref[...] = seg(_P_UP, _P_NQ)
    qr = seg(_P_NQ, _P_KV)
    q_ref[...] = qr * lax.rsqrt(_segmean(qr * qr, e256_ref) + EPS) * qn_ref[...]
    kv = seg(_P_KV, _P_NG)
    k_sel = kv[:, 256:384]
    k_sel = k_sel * lax.rsqrt(_segmean(k_sel * k_sel, e128_ref) + EPS) * kn1_ref[...]
    k_win = kv[:, 512:640]
    k_win = k_win * lax.rsqrt(_segmean(k_win * k_win, e128_ref) + EPS) * kn2_ref[...]
    v_sel, v_win = kv[:, 384:512], kv[:, 640:768]
    rows_ref[:, 0:256] = kv[:, 0:256]
    rows_ref[:, 256:384] = k_sel
    rows_ref[:, 384:512] = v_sel
    win_ref[:, 0:128] = k_win
    win_ref[:, 128:256] = v_win
    ng_ref[...] = jax.nn.sigmoid(seg(_P_NG, _P_END) + gb_ref[...])
    if t_refs:
        ksb_ref, svt_ref, kwb_ref, wvt_ref = t_refs
        ksb_ref[...] = k_sel.astype(BF16)
        svt_ref[...] = v_sel.T.astype(BF16)
        kwb_ref[...] = k_win.astype(BF16)
        wvt_ref[...] = v_win.T.astype(BF16)


def _proj_call(x, lw, with_t):
    n = x.shape[0]
    tm = min(512, n)
    row = lambda w: pl.BlockSpec((tm, w), lambda i: (i, 0))
    consts = (lw['g1'], lw['wp'], lw['wa2'], lw['ba'], lw['qn'], lw['kn1'], lw['kn2'], lw['gb'], lw['e256'], lw['e128'])
    widths = (512, 256, 256, 256, 128, 256, 256, 512, 256, 128)
    out_shape = [jax.ShapeDtypeStruct((n, w), F32) for w in widths]
    out_specs = [row(w) for w in widths]
    if with_t:
        col = pl.BlockSpec((LANES, tm), lambda i: (0, i))
        out_shape += [jax.ShapeDtypeStruct((n, LANES), BF16), jax.ShapeDtypeStruct((LANES, n), BF16)] * 2
        out_specs += [row(LANES), col] * 2
    return pl.pallas_call(
        _proj_kernel, grid=(n // tm,),
        in_specs=[row(D_MODEL)] + [_const_spec(c.shape) for c in consts],
        out_specs=out_specs, out_shape=out_shape,
        compiler_params=_params(("parallel",)), name="proj",
    )(x, *consts)


_CONV_PAD = 32


def _conv_kernel(a_ref, prev_ref, w_ref, b_ref, lg_ref, lb_ref, y_ref, u_ref, ext_ref, z_ref, *, tm):
    zrows = z_ref.shape[0]

    @pl.when(pl.program_id(1) == 0)
    def _():
        ext_ref[0:_CONV_PAD, :] = prev_ref[0]
        tail = ext_ref.shape[0] - (_CONV_PAD + tm)
        ext_ref[_CONV_PAD + tm:, :] = jnp.zeros((tail, CONV_CH), F32)

    a = a_ref[0]
    u = a[:, :CONV_CH] * jax.nn.sigmoid(a[:, CONV_CH:])
    ext_ref[_CONV_PAD:_CONV_PAD + tm, :] = u
    acc = jnp.zeros((tm, CONV_CH), F32) + b_ref[...]
    first = _CONV_PAD - (CONV_WIDTH - 1)
    for b in range(SUBLANES):
        z = None
        for j in range(b, CONV_WIDTH, SUBLANES):
            term = ext_ref[j - b:j - b + zrows, :] * w_ref[j:j + 1, :]
            z = term if z is None else z + term
        z_ref[...] = z
        acc = acc + z_ref[first + b:first + b + tm, :]
    mu = jnp.mean(acc, axis=-1, keepdims=True)
    cen = acc - mu
    var = jnp.mean(cen * cen, axis=-1, keepdims=True)
    yn = cen * lax.rsqrt(var + EPS) * lg_ref[...] + lb_ref[...]
    y_ref[0] = yn * jax.nn.sigmoid(yn)
    u_ref[0] = u
    ext_ref[0:_CONV_PAD, :] = ext_ref[tm:tm + _CONV_PAD, :]


def _conv_call(a, prev, lw):
    b, t, _ = a.shape
    tm = min(512, t)
    prev = jnp.pad(prev, ((0, 0), (_CONV_PAD - (CONV_WIDTH - 1), 0), (0, 0)))
    blk = lambda w: pl.BlockSpec((1, tm, w), lambda i, j: (i, j, 0))
    consts = (lw['cw'], lw['cb'], lw['clg'], lw['clb'])
    return pl.pallas_call(
        functools.partial(_conv_kernel, tm=tm), grid=(b, t // tm),
        in_specs=[blk(2 * CONV_CH), pl.BlockSpec((1, _CONV_PAD, CONV_CH), lambda i, j: (i, 0, 0))]
        + [_const_spec(c.shape) for c in consts],
        out_specs=[blk(CONV_CH), blk(CONV_CH)],
        out_shape=[jax.ShapeDtypeStruct((b, t, CONV_CH), F32)] * 2,
        scratch_shapes=[pltpu.VMEM((_CONV_PAD + tm + 2 * SUBLANES, CONV_CH), F32),
                        pltpu.VMEM((tm + 2 * SUBLANES, CONV_CH), F32)],
        compiler_params=_params(("arbitrary", "arbitrary")), name="conv",
    )(a, prev, *consts)


_POOL_PAD = 16


def _pool_kernel(u_ref, prev_ref, w_ref, sc_ref, y_ref, ext_ref, *, tm, pos0):
    t = pl.program_id(1)

    @pl.when(t == 0)
    def _():
        ext_ref[0:_POOL_PAD, :] = prev_ref[0]

    u = u_ref[0]
    ext_ref[_POOL_PAD:_POOL_PAD + tm, :] = u

    def back(d):
        return ext_ref[_POOL_PAD - d:_POOL_PAD - d + tm, :]

    sums, acc, d = [], u, 1
    for w in POOL_WINDOWS:
        while d < w:
            acc = acc + back(d)
            d += 1
        sums.append(acc)
    grp = lax.broadcasted_iota(jnp.int32, (tm, POOL_CH), 1) // (POOL_CH // len(POOL_WINDOWS))
    pos = pos0 + t * tm + lax.broadcasted_iota(jnp.int32, (tm, POOL_CH), 0)
    win, width = sums[-1], jnp.full((tm, POOL_CH), POOL_WINDOWS[-1], jnp.int32)
    for gi in range(len(POOL_WINDOWS) - 2, -1, -1):
        win = jnp.where(grp == gi, sums[gi], win)
        width = jnp.where(grp == gi, POOL_WINDOWS[gi], width)
    cnt = jnp.minimum(pos + 1, width).astype(F32)
    pooled = win / cnt - u
    y_ref[0] = _bdot(pooled, w_ref[...]) * sc_ref[...]
    ext_ref[0:_POOL_PAD, :] = ext_ref[tm:tm + _POOL_PAD, :]


def _pool_call(u, prev, pos0, lw):
    b, t, _ = u.shape
    tm = min(512, t)
    prev = jnp.pad(prev, ((0, 0), (_POOL_PAD - POOL_STATE, 0), (0, 0)))
    blk = pl.BlockSpec((1, tm, POOL_CH), lambda i, j: (i, j, 0))
    return pl.pallas_call(
        functools.partial(_pool_kernel, tm=tm, pos0=pos0), grid=(b, t // tm),
        in_specs=[blk, pl.BlockSpec((1, _POOL_PAD, POOL_CH), lambda i, j: (i, 0, 0)),
                  _const_spec(lw['pw'].shape), _const_spec(lw['psc'].shape)],
        out_specs=blk, out_shape=jax.ShapeDtypeStruct((b, t, POOL_CH), F32),
        scratch_shapes=[pltpu.VMEM((_POOL_PAD + tm, POOL_CH), F32)],
        compiler_params=_params(("arbitrary", "arbitrary")), name="pool",
    )(u, prev, lw['pw'], lw['psc'])


def _gla_kernel(qk_ref, v_ref, la_ref, og_ref, s0_ref, on_ref, eh_ref, mbd_ref, e256_ref,
                y_ref, sfin_ref, st_ref, kext, bext, vext, qs_s, ks_s, g_s, o_s, *, tm):
    R = GLA_SUB
    t = pl.program_id(1)
    kw = GLA_HEADS * GLA_DK

    @pl.when(t == 0)
    def _():
        st_ref[...] = s0_ref[0]
        kext[0:R, :] = jnp.zeros((R, kw), F32)
        bext[0:R, :] = jnp.zeros((R, kw), F32)
        vext[0:R, :] = jnp.zeros((R, GLA_HEADS * GLA_DV), F32)

    qk = qk_ref[0]
    q = qk[:, :kw] * (GLA_DK ** -0.5)
    k = qk[:, kw:]
    v = v_ref[0]
    la = la_ref[0]
    r = lax.broadcasted_iota(jnp.int32, (tm, kw), 0) % R
    b = la
    c = la
    for s in (1, 2, 4, 8):
        b = b + jnp.where(r >= s, pltpu.roll(b, s, axis=0), 0.0)
        c = c + jnp.where(r < R - s, pltpu.roll(c, tm - s, axis=0), 0.0)
    kext[R:R + tm, :] = k
    bext[R:R + tm, :] = b
    vext[R:R + tm, :] = v
    o = jnp.zeros((tm, GLA_HEADS * GLA_DV), F32)
    for d in range(R):
        kd = kext[R - d:R - d + tm, :]
        bd = bext[R - d:R - d + tm, :]
        vd = vext[R - d:R - d + tm, :]
        e = jnp.exp(jnp.where(r >= d, b - bd, NEG))
        a = jnp.dot((q * kd * e).astype(BF16), eh_ref[...], preferred_element_type=F32)
        o = o + a * vd
    o_s[...] = o
    qs_s[...] = q * jnp.exp(b)
    ks_s[...] = k * jnp.exp(c - la)
    g_s[...] = jnp.exp(b + c - la)

    nsb = tm // R
    group = 4 if nsb % 4 == 0 else 1

    def body(i, carry):
        st = st_ref[...]
        for u in range(group):
            r0 = pl.multiple_of((i * group + u) * R, R)
            o_s[pl.ds(r0, R), :] += _nt(qs_s[pl.ds(r0, R), :].astype(BF16), st.astype(BF16))
            upd = _tn(vext[pl.ds(R + r0, R), :].astype(BF16), ks_s[pl.ds(r0, R), :].astype(BF16))
            st = st * g_s[pl.ds(r0, 1), :] + upd * mbd_ref[...]
        st_ref[...] = st
        return carry

    lax.fori_loop(0, nsb // group, body, 0)
    o = o_s[...]
    on = o * lax.rsqrt(_segmean(o * o, e256_ref) + EPS) * on_ref[...]
    og = og_ref[0]
    y_ref[0] = on * (og * jax.nn.sigmoid(og))

    @pl.when(t == pl.num_programs(1) - 1)
    def _():
        sfin_ref[0] = st_ref[...]


def _gla_call(qk, v, la, og, s0, lw):
    b, t, _ = qk.shape
    tp = -(-t // GLA_SUB) * GLA_SUB
    if tp != t:
        pad = lambda a: jnp.pad(a, ((0, 0), (0, tp - t), (0, 0)))
        qk, v, la, og = pad(qk), pad(v), pad(la), pad(og)
    tm = min(256, tp)
    kw, vw = GLA_HEADS * GLA_DK, GLA_HEADS * GLA_DV
    eye = jnp.eye(GLA_HEADS, dtype=F32)
    st0 = jnp.einsum('bhkv,hg->bhvgk', s0.astype(F32), eye).reshape(b, vw, kw)
    blk = lambda w: pl.BlockSpec((1, tm, w), lambda i, j: (i, j, 0))
    st_spec = pl.BlockSpec((1, vw, kw), lambda i, j: (i, 0, 0))
    consts = (lw['on'], lw['eh'], lw['mbd'], lw['e256'])
    y, st = pl.pallas_call(
        functools.partial(_gla_kernel, tm=tm), grid=(b, tp // tm),
        in_specs=[blk(2 * kw), blk(vw), blk(kw), blk(vw), st_spec] + [_const_spec(c.shape) for c in consts],
        out_specs=[blk(vw), st_spec],
        out_shape=[jax.ShapeDtypeStruct((b, tp, vw), F32), jax.ShapeDtypeStruct((b, vw, kw), F32)],
        scratch_shapes=[pltpu.VMEM((vw, kw), F32),
                        pltpu.VMEM((GLA_SUB + tm, kw), F32), pltpu.VMEM((GLA_SUB + tm, kw), F32),
                        pltpu.VMEM((GLA_SUB + tm, vw), F32),
                        pltpu.VMEM((tm, kw), F32), pltpu.VMEM((tm, kw), F32), pltpu.VMEM((tm, kw), F32),
                        pltpu.VMEM((tm, vw), F32)],
        compiler_params=_params(("arbitrary", "arbitrary")), name="gla",
    )(qk, v, la, og, st0, *consts)
    st = st.reshape(b, GLA_HEADS, GLA_DV, GLA_HEADS, GLA_DK)
    s_fin = jnp.stack([st[:, h, :, h, :] for h in range(GLA_HEADS)], axis=1).swapaxes(-1, -2)
    return y[:, :t], s_fin


def _cmp_copy(pt_ref, cache_ref, buf, sem, step, slot, j, kind, n_steps, page, feature_major):
    bb, ss = step // n_steps, step % n_steps
    pg = pt_ref[bb, ss * CMP_PAGES + j]
    src = (cache_ref.at[pg, pl.ds(kind * LANES, LANES), :] if feature_major
           else cache_ref.at[pg, :, pl.ds(kind * LANES, LANES)])
    return pltpu.make_async_copy(src, buf.at[slot, kind, pl.ds(j * page, page), :], sem.at[slot])


def _compress_kernel(pt_ref, cache_ref, wk_ref, wv_ref, pek_ref, pev_ref, kn0_ref, e128_ref,
                     kch_ref, kcl_ref, vct_ref, buf, sem, f_s, bias_s, xs, *, n_steps, page, total, feature_major):
    copy = functools.partial(_cmp_copy, pt_ref, cache_ref, buf, sem, n_steps=n_steps, page=page,
                             feature_major=feature_major)
    bi, si = pl.program_id(0), pl.program_id(1)
    step = bi * n_steps + si
    slot = step % 2
    rows = CMP_PAGES * page
    nchunk = rows // CMP_STRIDE

    def start(st, sl):
        for j in range(CMP_PAGES):
            for kind in range(2):
                copy(st, sl, j, kind).start()

    @pl.when(step == 0)
    def _():
        start(step, slot)
        rid = lax.broadcasted_iota(jnp.int32, (CMP_STRIDE, 2 * LANES), 0)
        bk = jnp.zeros((CMP_STRIDE, 2 * LANES), F32)
        bv = jnp.zeros((CMP_STRIDE, 2 * LANES), F32)
        for l in range(CMP_STRIDE):
            pk, pv = pek_ref[...], pev_ref[...]
            w_k = wk_ref[l // 2][(l % 2) * LANES:(l % 2 + 1) * LANES, :]
            w_v = wv_ref[l // 2][(l % 2) * LANES:(l % 2 + 1) * LANES, :]
            tk = jnp.concatenate([_bdot(pk[:, :LANES], w_k[:, :LANES]), _bdot(pk[:, LANES:], w_k[:, LANES:])], axis=1)
            tv = jnp.concatenate([_bdot(pv[:, :LANES], w_v[:, :LANES]), _bdot(pv[:, LANES:], w_v[:, LANES:])], axis=1)
            bk = bk + jnp.where(rid == l, tk, 0.0)
            bv = bv + jnp.where(rid == l, tv, 0.0)
        bias_s[0:1, :] = jnp.sum(bk, axis=0, keepdims=True)
        bias_s[1:2, :] = jnp.sum(bv, axis=0, keepdims=True)

    @pl.when(step + 1 < total)
    def _():
        start(step + 1, 1 - slot)

    for j in range(CMP_PAGES):
        for kind in range(2):
            copy(step, slot, j, kind).wait()
    pitch = xs.shape[1] // CMP_STRIDE
    if feature_major:
        for j in range(CMP_PAGES):
            for kind in range(2):
                x = buf[slot, kind, j * page:(j + 1) * page, :].T
                for v in range(page // SUBLANES):
                    m, l0 = j * (page // CMP_STRIDE) + (v * SUBLANES) // CMP_STRIDE, (v * SUBLANES) % CMP_STRIDE
                    xs[kind, pl.ds(l0 * pitch + m, SUBLANES, stride=pitch), :] = x[v * SUBLANES:(v + 1) * SUBLANES, :]

    def rows_of(kind, l):
        if feature_major:
            return xs[kind, l * pitch:l * pitch + nchunk, :].astype(BF16)
        return buf[slot, kind, pl.ds(l, nchunk, stride=CMP_STRIDE), :].astype(BF16)

    fk = jnp.zeros((nchunk, 2 * LANES), F32)
    fv = jnp.zeros((nchunk, 2 * LANES), F32)
    for l in range(0, CMP_STRIDE, 2):
        xk = jnp.concatenate([rows_of(0, l), rows_of(0, l + 1)], axis=1)
        xv = jnp.concatenate([rows_of(1, l), rows_of(1, l + 1)], axis=1)
        fk = fk + jnp.dot(xk, wk_ref[l // 2], preferred_element_type=F32)
        fv = fv + jnp.dot(xv, wv_ref[l // 2], preferred_element_type=F32)
    m0 = pl.multiple_of(si * nchunk, nchunk)
    f_s[pl.ds(m0, nchunk), 0:2 * LANES] = fk
    f_s[pl.ds(m0, nchunk), 2 * LANES:4 * LANES] = fv

    @pl.when(si == n_steps - 1)
    def _():
        ncp = f_s.shape[0]
        last = lax.broadcasted_iota(jnp.int32, (ncp, LANES), 0) == ncp - 1
        kc = f_s[:, 0:LANES] + pltpu.roll(f_s[:, LANES:2 * LANES], ncp - 1, axis=0)
        kc = kc + bias_s[0:1, 0:LANES] + bias_s[0:1, LANES:2 * LANES]
        kc = jnp.where(last, 0.0, kc)
        kc = kc * lax.rsqrt(_segmean(kc * kc, e128_ref) + EPS) * kn0_ref[...]
        hi, lo = _split(kc)
        kch_ref[0] = hi
        kcl_ref[0] = lo
        vc = f_s[:, 2 * LANES:3 * LANES] + pltpu.roll(f_s[:, 3 * LANES:4 * LANES], ncp - 1, axis=0)
        vc = vc + bias_s[1:2, 0:LANES] + bias_s[1:2, LANES:2 * LANES]
        vct_ref[0] = jnp.where(last, 0.0, vc).T.astype(BF16)


def _compress_call(cache, page_table, lw, feature_major):
    b, n_pages = page_table.shape
    page = cache.shape[2] if feature_major else cache.shape[1]
    assert n_pages % CMP_PAGES == 0 and page % CMP_STRIDE == 0 and (page == LANES or not feature_major)
    n_steps = n_pages // CMP_PAGES
    ncp = n_pages * page // CMP_STRIDE
    rows = CMP_PAGES * page
    consts = (lw['wk'], lw['wv'], lw['pek'], lw['pev'], lw['kn0'], lw['e128'])
    out3 = lambda s: pl.BlockSpec((1,) + s, lambda i, j, pt: (i, 0, 0))
    grid_spec = pltpu.PrefetchScalarGridSpec(
        num_scalar_prefetch=1, grid=(b, n_steps),
        in_specs=[pl.BlockSpec(memory_space=pl.ANY)]
        + [pl.BlockSpec(c.shape, lambda i, j, pt, nd=c.ndim: (0,) * nd) for c in consts],
        out_specs=[out3((ncp, LANES)), out3((ncp, LANES)), out3((LANES, ncp))],
        scratch_shapes=[pltpu.VMEM((2, 2, rows, LANES), F32), pltpu.SemaphoreType.DMA((2,)),
                        pltpu.VMEM((ncp, 4 * LANES), F32), pltpu.VMEM((SUBLANES, 2 * LANES), F32),
                        pltpu.VMEM((2, CMP_STRIDE * (rows // CMP_STRIDE + SUBLANES), LANES), F32)])
    return pl.pallas_call(
        functools.partial(_compress_kernel, n_steps=n_steps, page=page, total=b * n_steps,
                          feature_major=feature_major),
        grid_spec=grid_spec,
        out_shape=[jax.ShapeDtypeStruct((b, ncp, LANES), BF16), jax.ShapeDtypeStruct((b, ncp, LANES), BF16),
                   jax.ShapeDtypeStruct((b, LANES, ncp), BF16)],
        compiler_params=_params(("arbitrary", "arbitrary")), name="compress",
    )(page_table, cache, *consts)


def _softmax_cols(s, mask):
    sm = jnp.where(mask, s, NEG)
    m = jnp.max(sm, axis=0, keepdims=True)
    p = jnp.where(mask, jnp.exp(sm - m), 0.0)
    return p, m, jnp.sum(p, axis=0, keepdims=True)


def _flash_update(carry, s, mask, vt):
    m, l, acc = carry
    sm = jnp.where(mask, s, NEG)
    m_new = jnp.maximum(m, jnp.max(sm, axis=0, keepdims=True))
    alpha = jnp.exp(m - m_new)
    p = jnp.where(mask, jnp.exp(sm - m_new), 0.0)
    l = alpha * l + jnp.sum(p, axis=0, keepdims=True)
    acc = acc * alpha + jnp.dot(vt, p.astype(BF16), preferred_element_type=F32)
    return m_new, l, acc


def _compressed_branch(kch, kcl, vct, qp, colpos):
    qh, ql = _split(qp)
    s = _nt(kch, qh) + _nt(kch, ql) + _nt(kcl, qh)
    n_idx = lax.broadcasted_iota(jnp.int32, s.shape, 0)
    mask = n_idx <= jnp.right_shift(colpos - (2 * CMP_STRIDE - 1), CMP_STRIDE.bit_length() - 1)
    p, _, l = _softmax_cols(s, mask)
    p = p / jnp.maximum(l, 1e-30)
    return p, jnp.dot(vct, p.astype(BF16), preferred_element_type=F32)


def _block_scores(mbt, imp, colpos):
    ih, il = _split(imp)
    blk = jnp.dot(mbt, ih, preferred_element_type=F32) + jnp.dot(mbt, il, preferred_element_type=F32)
    j_idx = lax.broadcasted_iota(jnp.int32, blk.shape, 0)
    avail = j_idx * SEL_BLOCK <= colpos
    forced = (j_idx == colpos // SEL_BLOCK) | (j_idx == 0)
    return jnp.where(avail, jnp.where(forced, FORCE, blk), NEG)


def _pick_round(x, j_f):
    m = jnp.max(x, axis=0, keepdims=True)
    first = jnp.min(jnp.where(x == m, j_f, 1e9), axis=0, keepdims=True)
    pick = j_f == first
    return m, first, pick, jnp.where(pick, -jnp.inf, x)


def _head_rows_to_lanes(o_rows, gates, rows_per_head):
    n = rows_per_head
    lo_half = lax.broadcasted_iota(jnp.int32, (n, LANES), 1) < NSA_HD
    outs = []
    for h in range(NSA_HEADS):
        acc = None
        for c, o in enumerate(o_rows):
            term = gates[:, 3 * h + c:3 * h + c + 1] * o[h * n:(h + 1) * n, :]
            acc = term if acc is None else acc + term
        outs.append(acc)
    y01 = jnp.where(lo_half, outs[0], pltpu.roll(outs[1], NSA_HD, axis=1))
    y23 = jnp.where(lo_half, pltpu.roll(outs[2], NSA_HD, axis=1), outs[3])
    return jnp.concatenate([y01, y23], axis=1)


def _nsa_prompt_kernel(q_ref, ng_ref, kch_ref, kcl_ref, vct_ref, mbt_ref, ks_ref, svt_ref, kw_ref, wvt_ref,
                       e16_ref, y_ref, sel_s, oc_s, *, n_variants):
    qb_rows = Q_BLOCK
    s0 = pl.program_id(1) * qb_rows
    q = q_ref[0]
    lo_half = lax.broadcasted_iota(jnp.int32, (qb_rows, LANES), 1) < NSA_HD
    q01, q23 = q[:, :LANES], q[:, LANES:]
    qp = jnp.concatenate([jnp.where(lo_half, q01, 0.0),
                          jnp.where(lo_half, pltpu.roll(q01, NSA_HD, axis=1), 0.0),
                          jnp.where(lo_half, 0.0, pltpu.roll(q23, NSA_HD, axis=1)),
                          jnp.where(lo_half, 0.0, q23)], axis=0)
    ncols = NSA_HEADS * qb_rows
    colpos = s0 + lax.broadcasted_iota(jnp.int32, (1, ncols), 1) % qb_rows
    qb = qp.astype(BF16)

    ncp, nbp = kch_ref.shape[1], sel_s.shape[0]

    def select(nc, nb):
        p, oc_t = _compressed_branch(kch_ref[0, :nc, :], kcl_ref[0, :nc, :], vct_ref[0, :, :nc], qp, colpos)
        oc_s[...] = oc_t
        imp = jnp.concatenate([p[:, 0:qb_rows] + p[:, qb_rows:2 * qb_rows],
                               p[:, 2 * qb_rows:3 * qb_rows] + p[:, 3 * qb_rows:]], axis=1)
        x0 = _block_scores(mbt_ref[:nb, :nc], imp, colpos[:, :2 * qb_rows])
        j_f = lax.broadcasted_iota(jnp.int32, x0.shape, 0).astype(F32)
        x = x0
        for _ in range(N_SEL):
            _, _, _, x = _pick_round(x, j_f)
        sel = jnp.where((x == -jnp.inf) & (x0 > 0.5 * NEG), 0.0, NEG)
        sel_s[0:nb, :] = jnp.concatenate([sel[:, :qb_rows], sel[:, :qb_rows], sel[:, qb_rows:], sel[:, qb_rows:]], axis=1)

    variant = pl.program_id(1) // (pl.num_programs(1) // n_variants)
    for v in range(n_variants):
        pl.when(variant == v)(functools.partial(select, ncp * (v + 1) // n_variants, nbp * (v + 1) // n_variants))
    oc_t = oc_s[...]

    kt = SEL_TILE
    nblk = kt // SEL_BLOCK
    qpt = qp.T.astype(BF16)
    pad_rows = jnp.zeros((LANES - nblk, ncols), BF16)

    def queries(c):
        bias = sel_s[pl.ds(pl.multiple_of(c * nblk, nblk), nblk), :].astype(BF16)
        return jnp.concatenate([qpt, bias, pad_rows], axis=0)

    def scores(c):
        k0 = pl.multiple_of(c * kt, kt)
        keys = jnp.concatenate([ks_ref[0, pl.ds(k0, kt), :], e16_ref[...]], axis=1)
        return k0, jnp.dot(keys, queries(c), preferred_element_type=F32)

    def unmasked(carry, k0, s):
        m, l, acc = carry
        m_new = jnp.maximum(m, jnp.max(s, axis=0, keepdims=True))
        alpha = jnp.exp(m - m_new)
        p = jnp.exp(s - m_new)
        l = alpha * l + jnp.sum(p, axis=0, keepdims=True)
        acc = acc * alpha + jnp.dot(svt_ref[0, :, pl.ds(k0, kt)], p.astype(BF16), preferred_element_type=F32)
        return m_new, l, acc

    def pair(c, carry):
        ka, sa = scores(2 * c)
        kb, sb = scores(2 * c + 1)
        return unmasked(unmasked(carry, ka, sa), kb, sb)

    def single(c, carry):
        return unmasked(carry, *scores(c))

    init = (jnp.full((1, ncols), NEG, F32), jnp.zeros((1, ncols), F32), jnp.zeros((LANES, ncols), F32))
    last = (s0 + qb_rows - 1) // kt
    carry = lax.fori_loop(0, last // 2, pair, init)
    carry = lax.fori_loop(2 * (last // 2), last, single, carry)

    sub = 4 * Q_BLOCK
    k_diag = pl.multiple_of(last * kt, kt)
    q_diag = queries(last)

    def diagonal(j, carry):
        off = pl.multiple_of(j * sub, sub)
        k0 = pl.multiple_of(k_diag + off, sub)
        keys = jnp.concatenate([ks_ref[0, pl.ds(k0, sub), :], e16_ref[pl.ds(off, sub), :]], axis=1)
        s = jnp.dot(keys, q_diag, preferred_element_type=F32)
        visible = lax.broadcasted_iota(jnp.int32, s.shape, 0) <= colpos - k0
        return _flash_update(carry, s, visible, svt_ref[0, :, pl.ds(k0, sub)])

    _, l, acc = lax.fori_loop(0, (s0 + qb_rows - k_diag + sub - 1) // sub, diagonal, carry)
    os_t = acc / jnp.maximum(l, 1e-30)

    span = WINDOW + qb_rows
    w0 = pl.multiple_of(jnp.maximum(s0 - WINDOW, 0), qb_rows)
    s = _nt(kw_ref[0, pl.ds(w0, span), :], qb)
    krow = lax.broadcasted_iota(jnp.int32, s.shape, 0)
    newest = colpos - w0
    pw, _, lw_ = _softmax_cols(s, (krow <= newest) & (krow > newest - WINDOW))
    ow_t = jnp.dot(wvt_ref[0, :, pl.ds(w0, span)], pw.astype(BF16), preferred_element_type=F32) / jnp.maximum(lw_, 1e-30)

    y_ref[0] = _head_rows_to_lanes([oc_t.T, os_t.T, ow_t.T], ng_ref[0], qb_rows)


def _nsa_prompt_call(q, ng, kch, kcl, vct, ksb, svt, kwb, wvt, lw):
    b, t, _ = q.shape
    assert t % SEL_TILE == 0 and t >= WINDOW + Q_BLOCK
    ncp = kch.shape[1]
    nbp = t // SEL_BLOCK
    n_variants = max(v for v in (1, 2, 4) if (t // v) % SEL_TILE == 0 and (ncp // v) % LANES == 0)
    full = lambda s: pl.BlockSpec((1,) + s, lambda i, j: (i, 0, 0))
    blk = lambda w: pl.BlockSpec((1, Q_BLOCK, w), lambda i, j: (i, j, 0))
    return pl.pallas_call(
        functools.partial(_nsa_prompt_kernel, n_variants=n_variants), grid=(b, t // Q_BLOCK),
        in_specs=[blk(2 * LANES), blk(LANES), full((ncp, LANES)), full((ncp, LANES)), full((LANES, ncp)),
                  _const_spec(lw['mbt_p'].shape), full((t, LANES)), full((LANES, t)), full((t, LANES)),
                  full((LANES, t)), _const_spec(lw['e16'].shape)],
        out_specs=blk(2 * LANES), out_shape=jax.ShapeDtypeStruct((b, t, 2 * LANES), F32),
        scratch_shapes=[pltpu.VMEM((nbp, NSA_HEADS * Q_BLOCK), F32), pltpu.VMEM((LANES, NSA_HEADS * Q_BLOCK), F32)],
        compiler_params=_params(("arbitrary", "arbitrary")), name="nsa_prompt",
    )(q, ng, kch, kcl, vct, lw['mbt_p'], ksb, svt, kwb, wvt, lw['e16'])


_SCOL = SUBLANES


def _nsa_sample_a_kernel(qp_ref, kch_ref, kcl_ref, vct_ref, mbt_ref, cwin_ref, wnew_ref,
                         oc_ref, ow_ref, idx_ref, ok_ref, *, past_len, n_dec):
    qp = qp_ref[0]
    qb = qp.astype(BF16)
    col = lax.broadcasted_iota(jnp.int32, (1, LANES), 1)
    tcol = col % _SCOL
    colpos = past_len + tcol
    p, oc_t = _compressed_branch(kch_ref[0], kcl_ref[0], vct_ref[0], qp, colpos)
    oc_ref[0] = oc_t
    first_head = (col % (2 * _SCOL)) < _SCOL
    imp = p + jnp.where(first_head, pltpu.roll(p, LANES - _SCOL, axis=1), pltpu.roll(p, _SCOL, axis=1))
    x = _block_scores(mbt_ref[...], imp, colpos)
    j_f = lax.broadcasted_iota(jnp.int32, x.shape, 0).astype(F32)
    for rd in range(N_SEL - 1):
        m, first, _, x = _pick_round(x, j_f)
        idx_ref[0, rd:rd + 1, :] = first.astype(jnp.int32)
        ok_ref[0, rd:rd + 1, :] = jnp.where(m > 0.5 * NEG, 1.0, 0.0)
    idx_ref[0, N_SEL - 1:N_SEL, :] = jnp.zeros((1, LANES), jnp.int32)
    ok_ref[0, N_SEL - 1:N_SEL, :] = jnp.ones((1, LANES), F32)

    kp, kn = cwin_ref[0], wnew_ref[0]
    wbuf = kp.shape[0]
    s1 = _nt(kp[:, :LANES].astype(BF16), qb)
    kpos1 = past_len - wbuf + lax.broadcasted_iota(jnp.int32, s1.shape, 0)
    d1 = colpos - kpos1
    mask1 = (d1 >= 0) & (d1 < WINDOW) & (kpos1 >= 0)
    s2 = _nt(kn[:, :LANES].astype(BF16), qb)
    kidx = lax.broadcasted_iota(jnp.int32, s2.shape, 0)
    mask2 = (kidx <= tcol) & (kidx < n_dec)
    carry = (jnp.full((1, LANES), NEG, F32), jnp.zeros((1, LANES), F32), jnp.zeros((LANES, LANES), F32))
    carry = _flash_update(carry, s1, mask1, kp[:, LANES:].T.astype(BF16))
    _, l, acc = _flash_update(carry, s2, mask2, kn[:, LANES:].T.astype(BF16))
    ow_ref[0] = acc / jnp.maximum(l, 1e-30)


def _nsa_sample_a_call(qp, kch, kcl, vct, cwin, wnew, lw, past_len, n_dec):
    b = qp.shape[0]
    ncp = kch.shape[1]
    full = lambda a: pl.BlockSpec((1,) + a.shape[1:], lambda i: (i, 0, 0))
    sq = pl.BlockSpec((1, LANES, LANES), lambda i: (i, 0, 0))
    rnd = pl.BlockSpec((1, N_SEL, LANES), lambda i: (i, 0, 0))
    return pl.pallas_call(
        functools.partial(_nsa_sample_a_kernel, past_len=past_len, n_dec=n_dec), grid=(b,),
        in_specs=[sq, full(kch), full(kcl), full(vct), _const_spec(lw['mbt_s'].shape), full(cwin), full(wnew)],
        out_specs=[sq, sq, rnd, rnd],
        out_shape=[jax.ShapeDtypeStruct((b, LANES, LANES), F32)] * 2
        + [jax.ShapeDtypeStruct((b, N_SEL, LANES), jnp.int32), jax.ShapeDtypeStruct((b, N_SEL, LANES), F32)],
        compiler_params=_params(("parallel",)), name="nsa_sample_a",
    )(qp, kch, kcl, vct, lw['mbt_s'], cwin, wnew)


def _sel_copy(pt_ref, info_ref, cache_ref, kvbuf, sem, bi, slot, c, rd, n_dec, page):
    j = info_ref[bi, c * (N_SEL - 1) + rd]
    pg = pt_ref[bi, j // (page // SEL_BLOCK)]
    return pltpu.make_async_copy(cache_ref.at[pg, pl.ds(2, 2), c // n_dec],
                                 kvbuf.at[slot, c, :, :, pl.ds(rd * page, page)], sem.at[slot])


def _nsa_sample_b_kernel(pt_ref, info_ref, cache_ref, qpt_ref, newt_ref, oc_ref, ow_ref, ngt_ref,
                         y_ref, kvbuf, sem, *, n_dec, page):
    bi = pl.program_id(0)
    slot = bi % 2
    ncomb = NSA_KV * n_dec
    nsel = N_SEL - 1
    past_keys = nsel * page
    copy = functools.partial(_sel_copy, pt_ref, info_ref, cache_ref, kvbuf, sem, n_dec=n_dec, page=page)

    def start(b, sl):
        for c in range(ncomb):
            for rd in range(nsel):
                copy(b, sl, c, rd).start()

    @pl.when(bi == 0)
    def _():
        start(bi, slot)

    @pl.when(bi + 1 < pl.num_programs(0))
    def _():
        start(bi + 1, 1 - slot)

    newt = newt_ref[0]
    for c in range(ncomb):
        g = c // n_dec
        kvbuf[slot, c, 0, :, past_keys:past_keys + LANES] = newt[g * NSA_HD:(g + 1) * NSA_HD, :]
        kvbuf[slot, c, 1, :, past_keys:past_keys + LANES] = newt[LANES + g * NSA_HD:LANES + (g + 1) * NSA_HD, :]
    for c in range(ncomb):
        for rd in range(nsel):
            copy(bi, slot, c, rd).wait()

    qpt = qpt_ref[0]
    lane = lax.broadcasted_iota(jnp.int32, (1, LANES), 1)
    lane_p = lax.broadcasted_iota(jnp.int32, (1, page), 1)
    rep = NSA_HEADS // NSA_KV
    os_h = [jnp.zeros((NSA_HD, LANES), F32) for _ in range(NSA_HEADS)]
    for c in range(ncomb):
        g, t = divmod(c, n_dec)
        pieces = []
        for rd in range(nsel):
            j = info_ref[bi, c * nsel + rd]
            ok = info_ref[bi, ncomb * nsel + c * nsel + rd]
            chosen = (lane_p // SEL_BLOCK == j % (page // SEL_BLOCK)) & (ok > 0)
            pieces.append(jnp.where(chosen, 0.0, NEG))
        pieces.append(jnp.where(lane <= t, 0.0, NEG))
        bias = jnp.concatenate(pieces, axis=1)
        keys, vals = kvbuf[slot, c, 0], kvbuf[slot, c, 1]
        for r in range(rep):
            h = g * rep + r
            cidx = h * _SCOL + t
            qcol = qpt[g * NSA_HD:(g + 1) * NSA_HD, cidx:cidx + 1]
            s = jnp.sum(keys * qcol, axis=0, keepdims=True) + bias
            m = jnp.max(s, axis=1, keepdims=True)
            p = jnp.exp(s - m)
            l = jnp.sum(p, axis=1, keepdims=True)
            o = jnp.sum(vals * p, axis=1, keepdims=True) / jnp.maximum(l, 1e-30)
            os_h[h] = jnp.where(lane == t, o, os_h[h])
    oc, ow, ngt = oc_ref[0], ow_ref[0], ngt_ref[0]
    for h in range(NSA_HEADS):
        g = h // rep

        def to_front(a):
            blk = a[g * NSA_HD:(g + 1) * NSA_HD, :]
            return blk if h == 0 else pltpu.roll(blk, LANES - h * _SCOL, axis=1)

        y_ref[0, h * NSA_HD:(h + 1) * NSA_HD, :] = (ngt[3 * h:3 * h + 1, :] * to_front(oc)
                                                   + ngt[3 * h + 1:3 * h + 2, :] * os_h[h]
                                                   + ngt[3 * h + 2:3 * h + 3, :] * to_front(ow))


def _nsa_sample_b_call(page_table, info, cache_t, qpt, newt, oc, ow, ngt, n_dec):
    b = qpt.shape[0]
    page = cache_t.shape[2]
    cache_t = cache_t.reshape(cache_t.shape[0], 4, NSA_KV, NSA_HD, page)
    ncomb = NSA_KV * n_dec
    keys = (N_SEL - 1) * page + LANES
    spec = lambda a: pl.BlockSpec((1,) + a.shape[1:], lambda i, pt, sl: (i, 0, 0))
    grid_spec = pltpu.PrefetchScalarGridSpec(
        num_scalar_prefetch=2, grid=(b,),
        in_specs=[pl.BlockSpec(memory_space=pl.ANY), spec(qpt), spec(newt), spec(oc), spec(ow), spec(ngt)],
        out_specs=pl.BlockSpec((1, NSA_HEADS * NSA_HD, LANES), lambda i, pt, sl: (i, 0, 0)),
        scratch_shapes=[pltpu.VMEM((2, ncomb, 2, NSA_HD, keys), F32), pltpu.SemaphoreType.DMA((2,))])
    return pl.pallas_call(
        functools.partial(_nsa_sample_b_kernel, n_dec=n_dec, page=page), grid_spec=grid_spec,
        out_shape=jax.ShapeDtypeStruct((b, NSA_HEADS * NSA_HD, LANES), F32),
        compiler_params=_params(("arbitrary",)), name="nsa_sample_b",
    )(page_table, info, cache_t, qpt, newt, oc, ow, ngt)


def _merge_kernel(x_ref, g1_ref, wg_ref, ya_ref, yb_ref, yc_ref, yd_ref, wout_ref, wo_ref, o_ref):
    x = x_ref[...]
    h = _rms(x, g1_ref[...]).astype(BF16)
    mix = None
    for i, y_ref in enumerate((ya_ref, yb_ref, yc_ref, yd_ref)):
        gate = jax.nn.sigmoid(_nt(h, wg_ref[i * D_MODEL:(i + 1) * D_MODEL, :]))
        term = gate * _bdot(y_ref[...], wout_ref[i])
        mix = term if mix is None else mix + term
    o_ref[...] = x + _bdot(mix, wo_ref[...])


def _merge_call(x, ys, lw):
    n = x.shape[0]
    tm = min(512, n)
    row = lambda w: pl.BlockSpec((tm, w), lambda i: (i, 0))
    return pl.pallas_call(
        _merge_kernel, grid=(n // tm,),
        in_specs=[row(D_MODEL), _const_spec(lw['g1'].shape), _const_spec(lw['wgate'].shape)] + [row(2 * LANES)] * 4
        + [_const_spec(lw['wout'].shape), _const_spec(lw['wo'].shape)],
        out_specs=row(D_MODEL), out_shape=jax.ShapeDtypeStruct((n, D_MODEL), F32),
        compiler_params=_params(("parallel",)), name="merge",
    )(x, lw['g1'], lw['wgate'], *ys, lw['wout'], lw['wo'])


_FFN_CHUNK = 256


def _ffn_kernel(x_ref, g2_ref, wg_ref, wu_ref, wd_ref, o_ref):
    x = x_ref[...]
    h = _rms(x, g2_ref[...]).astype(BF16)
    acc = x
    for c in range(0, D_FF, _FFN_CHUNK):
        g = jnp.dot(h, wg_ref[:, c:c + _FFN_CHUNK], preferred_element_type=F32)
        u = jnp.dot(h, wu_ref[:, c:c + _FFN_CHUNK], preferred_element_type=F32)
        acc = acc + _bdot(g * jax.nn.sigmoid(g) * u, wd_ref[c:c + _FFN_CHUNK, :])
    o_ref[...] = acc


def _ffn_call(x, lw):
    n = x.shape[0]
    tm = min(512, n)
    row = pl.BlockSpec((tm, D_MODEL), lambda i: (i, 0))
    return pl.pallas_call(
        _ffn_kernel, grid=(n // tm,),
        in_specs=[row, _const_spec(lw['g2'].shape), _const_spec(lw['fg'].shape), _const_spec(lw['fu'].shape),
                  _const_spec(lw['fd'].shape)],
        out_specs=row, out_shape=jax.ShapeDtypeStruct((n, D_MODEL), F32),
        compiler_params=_params(("parallel",)), name="ffn",
    )(x, lw['g2'], lw['fg'], lw['fu'], lw['fd'])


def _block_diag(blocks):
    g, a, b = blocks.shape[-3:]
    eye = jnp.eye(g, dtype=blocks.dtype)
    out = jnp.einsum('...gab,gh->...gahb', blocks, eye)
    return out.reshape(blocks.shape[:-3] + (g * a, g * b))


def _seg_mean_matrix(width):
    seg = np.arange(width) // NSA_HD
    return jnp.asarray(np.where(seg[:, None] == seg[None, :], 1.0 / NSA_HD, 0.0), BF16)


def _band_matrix(nbp, ncp):
    ratio = SEL_BLOCK // CMP_STRIDE
    j = np.arange(nbp)[:, None]
    n = np.arange(ncp)[None, :]
    return jnp.asarray((n >= ratio * j - 1) & (n <= ratio * j + ratio - 1), BF16)


def _expand_matrix(nrows, nblk):
    return jnp.asarray(np.arange(nrows)[:, None] // SEL_BLOCK == np.arange(nblk)[None, :], BF16)


def _prep_layer(w, l, t_prompt, past_len):
    row = lambda a: a.reshape(1, -1).astype(F32)
    w_in_t = jnp.transpose(w['w_in'], (2, 0, 1))[:, l, :]
    pad_to = lambda a, n: jnp.pad(a, ((0, n - a.shape[0]), (0, 0)))
    wp = jnp.concatenate([w_in_t[0:1280], pad_to(w_in_t[1280:1296], LANES), w_in_t[1296:2576],
                          pad_to(w_in_t[2576:2588], LANES)], axis=0).astype(BF16)
    phi, pe = w['nsa_phi'][l], w['nsa_pe'][l]
    half = CMP_STRIDE

    def cmp_w(p):
        bd = lambda a: _block_diag(jnp.broadcast_to(a[:, None], (half, NSA_KV) + a.shape[1:]))
        per_l = jnp.concatenate([bd(p[:half]), bd(p[half:])], axis=-1)
        return per_l.reshape(half // 2, 2 * LANES, 2 * LANES).astype(BF16)

    def cmp_pe(p):
        return jnp.concatenate([jnp.tile(p[:half], (1, NSA_KV)), jnp.tile(p[half:], (1, NSA_KV))], axis=1).astype(F32)

    kw = GLA_HEADS * GLA_DK
    hk = np.arange(kw) // GLA_DK
    hv = np.arange(GLA_HEADS * GLA_DV) // GLA_DV
    same_head = hk[:, None] == hv[None, :]
    return {
        'g1': row(w['norm1'][l]), 'g2': row(w['norm2'][l]), 'wp': wp, 'wgate': w_in_t[2588:].astype(BF16),
        'wa2': jnp.pad(w['gla_wa2'][l], ((0, LANES - GLA_LOWRANK), (0, 0))).astype(BF16), 'ba': row(w['gla_ba'][l]),
        'qn': row(jnp.tile(w['nsa_qn'][l], NSA_HEADS)) * (NSA_HD ** -0.5),
        'kn0': row(jnp.tile(w['nsa_kn'][l][0], NSA_KV)), 'kn1': row(jnp.tile(w['nsa_kn'][l][1], NSA_KV)),
        'kn2': row(jnp.tile(w['nsa_kn'][l][2], NSA_KV)),
        'gb': row(jnp.pad(w['nsa_gb'][l], (0, LANES - 3 * NSA_HEADS))),
        'e256': _seg_mean_matrix(2 * LANES), 'e128': _seg_mean_matrix(LANES),
        'cw': jnp.pad(w['conv_w'][l], ((0, 1), (0, 0))).astype(F32), 'cb': row(w['conv_b'][l]),
        'clg': row(w['conv_ln_g'][l]), 'clb': row(w['conv_ln_b'][l]),
        'pw': _block_diag(w['pool_w'][l]).astype(BF16), 'psc': row(w['pool_scale'][l]),
        'on': row(jnp.tile(w['gla_onorm'][l], GLA_HEADS)),
        'eh': jnp.asarray(same_head, BF16), 'mbd': jnp.asarray(same_head.T, F32),
        'wk': cmp_w(phi[0]), 'wv': cmp_w(phi[1]), 'pek': cmp_pe(pe[0]), 'pev': cmp_pe(pe[1]),
        'mbt_p': _band_matrix(t_prompt // SEL_BLOCK, t_prompt // CMP_STRIDE),
        'mbt_s': _band_matrix(past_len // SEL_BLOCK, past_len // CMP_STRIDE),
        'e16': _expand_matrix(SEL_TILE, LANES),
        'wout': jnp.stack([w['w_out_conv'][l], w['w_out_gla'][l], w['w_out_pool'][l], w['w_out_nsa'][l]]).astype(BF16),
        'wo': w['w_o'][l].astype(BF16),
        'fg': w['ffn_gate'][l].astype(BF16), 'fu': w['ffn_up'][l].astype(BF16), 'fd': w['ffn_down'][l].astype(BF16),
    }


def _prompt_layer(x, lw):
    b, t, _ = x.shape
    xf = x.reshape(b * t, D_MODEL)
    a, qk, v, og, la, up, q, rows, win, ng, ksb, svt, kwb, wvt = _proj_call(xf, lw, True)
    r3 = lambda z: z.reshape(b, t, z.shape[-1])
    ya, u = _conv_call(r3(a), jnp.zeros((b, CONV_WIDTH - 1, CONV_CH), F32), lw)
    yc = _pool_call(r3(up), jnp.zeros((b, POOL_STATE, POOL_CH), F32), 0, lw)
    yb, gla_state = _gla_call(r3(qk), r3(v), r3(la), r3(og), jnp.zeros((b, GLA_HEADS, GLA_DK, GLA_DV), F32), lw)
    page = 2 * SEL_BLOCK
    cache = rows.reshape(b * t // page, page, rows.shape[-1])
    table = jnp.arange(b * t // page, dtype=jnp.int32).reshape(b, t // page)
    kch, kcl, vct = _compress_call(cache, table, lw, False)
    tcols =lambda z: z.reshape(LANES, b, t).swapaxes(0, 1)
    yd = _nsa_prompt_call(r3(q), r3(ng), kch, kcl, vct, r3(ksb), tcols(svt), r3(kwb), tcols(wvt), lw)
    x1 = _merge_call(xf, [ya.reshape(b * t, -1), yb.reshape(b * t, -1), yc.reshape(b * t, -1), yd.reshape(b * t, -1)], lw)
    x2 = _ffn_call(x1, lw).reshape(b, t, D_MODEL)
    keep = min(WINDOW, t)
    return (x2, r3(rows).reshape(b, t, 4, NSA_KV, NSA_HD), r3(win)[:, -keep:].reshape(b, keep, 2, NSA_KV, NSA_HD),
            u[:, -(CONV_WIDTH - 1):], r3(up)[:, -POOL_STATE:], gla_state)


def _sample_layer(x, lw, cache_t, cwin, conv_prev, pool_prev, gla_prev, page_table):
    b, t, _ = x.shape
    page = cache_t.shape[2]
    past_len = page_table.shape[1] * page
    assert past_len % SEL_BLOCK == 0 and t <= _SCOL and page % SEL_BLOCK == 0
    xf = x.reshape(b * t, D_MODEL)
    a, qk, v, og, la, up, q, rows, win, ng = _proj_call(xf, lw, False)
    r3 = lambda z: z.reshape(b, t, z.shape[-1])
    ya, u = _conv_call(r3(a), conv_prev, lw)
    yc = _pool_call(r3(up), pool_prev, past_len, lw)
    yb, gla_state = _gla_call(r3(qk), r3(v), r3(la), r3(og), gla_prev, lw)
    kch, kcl, vct = _compress_call(cache_t, page_table, lw, True)
    q4 = jnp.pad(r3(q).reshape(b, t, NSA_HEADS, NSA_HD), ((0, 0), (0, _SCOL - t), (0, 0), (0, 0))).swapaxes(1, 2)
    grp = jnp.arange(NSA_HEADS) // (NSA_HEADS // NSA_KV)
    onehot = (grp[:, None] == jnp.arange(NSA_KV)[None, :]).astype(F32)
    qp = jnp.einsum('bhtd,hg->bhtgd', q4, onehot).reshape(b, NSA_HEADS * _SCOL, LANES)
    qp = jnp.pad(qp, ((0, 0), (0, LANES - NSA_HEADS * _SCOL), (0, 0)))
    pad_rows = lambda z, n: jnp.pad(z, ((0, 0), (0, n - z.shape[1]), (0, 0)))
    cwin2 = cwin.reshape(b, cwin.shape[1], -1)
    oc, ow, idx, ok = _nsa_sample_a_call(qp, kch, kcl, vct, cwin2, pad_rows(r3(win), LANES), lw, past_len, t)
    cols = (jnp.arange(NSA_KV)[:, None] * (NSA_HEADS // NSA_KV) * _SCOL + jnp.arange(t)[None, :]).reshape(-1)
    per_comb = lambda z: z[:, :N_SEL - 1, :][:, :, cols].swapaxes(1, 2).reshape(b, -1)
    info = jnp.concatenate([per_comb(idx), per_comb(ok).astype(jnp.int32)], axis=1)
    lanes_t = lambda z, n: jnp.pad(z.swapaxes(1, 2), ((0, 0), (0, n - z.shape[2]), (0, LANES - z.shape[1])))
    yt = _nsa_sample_b_call(page_table, info, cache_t, qp.swapaxes(1, 2), lanes_t(r3(rows)[:, :, 2 * LANES:], 2 * LANES),
                            oc, ow, lanes_t(r3(ng)[:, :, :3 * NSA_HEADS], 2 * SUBLANES), t)
    yd = yt[:, :, :t].swapaxes(1, 2)
    x1 = _merge_call(xf, [ya.reshape(b * t, -1), yb.reshape(b * t, -1), yc.reshape(b * t, -1), yd.reshape(b * t, -1)], lw)
    x2 = _ffn_call(x1, lw).reshape(b, t, D_MODEL)
    win_all = jnp.concatenate([cwin2, r3(win)], axis=1)
    keep = min(WINDOW, win_all.shape[1])
    return (x2, r3(rows).reshape(b, t, 4, NSA_KV, NSA_HD), win_all[:, -keep:].reshape(b, keep, 2, NSA_KV, NSA_HD),
            jnp.concatenate([conv_prev, u], axis=1)[:, -(CONV_WIDTH - 1):],
            jnp.concatenate([pool_prev, r3(up)], axis=1)[:, -POOL_STATE:], gla_state)


def kernel(x_prompt, x_sample, cache_nsa_kv, cache_win_kv, state_conv, state_pool, state_gla, page_table, norm1, w_in, conv_w, conv_b, conv_ln_g, conv_ln_b, w_out_conv, gla_wa2, gla_ba, gla_onorm, w_out_gla, pool_w, pool_scale, w_out_pool, nsa_qn, nsa_kn, nsa_pe, nsa_phi, nsa_gb, w_out_nsa, w_o, norm2, ffn_gate, ffn_up, ffn_down):
    w = dict(norm1=norm1, w_in=w_in, conv_w=conv_w, conv_b=conv_b, conv_ln_g=conv_ln_g, conv_ln_b=conv_ln_b,
             w_out_conv=w_out_conv, gla_wa2=gla_wa2, gla_ba=gla_ba, gla_onorm=gla_onorm, w_out_gla=w_out_gla,
             pool_w=pool_w, pool_scale=pool_scale, w_out_pool=w_out_pool, nsa_qn=nsa_qn, nsa_kn=nsa_kn,
             nsa_pe=nsa_pe, nsa_phi=nsa_phi, nsa_gb=nsa_gb, w_out_nsa=w_out_nsa, w_o=w_o, norm2=norm2,
             ffn_gate=ffn_gate, ffn_up=ffn_up, ffn_down=ffn_down)
    depth, n_pool, page = cache_nsa_kv.shape[:3]
    past_len = page_table.shape[1] * page
    cache = jnp.moveaxis(cache_nsa_kv, 2, -1).reshape(depth * n_pool, -1, page)
    xp, xs = x_prompt, x_sample
    outs_p, outs_s = [], []
    for l in range(depth):
        lw = _prep_layer(w, l, xp.shape[1], past_len)
        xp, *st_p = _prompt_layer(xp, lw)
        xs, *st_s = _sample_layer(xs, lw, cache, cache_win_kv[l], state_conv[l], state_pool[l],
                                  state_gla[l], page_table + l * n_pool)
        outs_p.append(st_p)
        outs_s.append(st_s)
    stack = lambda outs, i: jnp.stack([o[i] for o in outs])
    return (xp, xs) + tuple(stack(outs_p, i) for i in range(5)) + tuple(stack(outs_s, i) for i in range(5))
```

```python
import functools

import numpy as np
import jax
import jax.numpy as jnp
from jax import lax
from jax.experimental import pallas as pl
from jax.experimental.pallas import tpu as pltpu

F32, BF16 = jnp.float32, jnp.bfloat16

D_MODEL = 1024
CONV_CH = 256
CONV_WIDTH = 31
GLA_HEADS, GLA_DK, GLA_DV = 4, 32, 64
GLA_LOWRANK = 16
GLA_TAU = 16.0
POOL_CH = 256
POOL_WINDOWS = (2, 4, 8, 16)
POOL_STATE = 15
NSA_HEADS, NSA_HD, NSA_KV = 4, 64, 2
CMP_STRIDE = 16
SEL_BLOCK = 64
N_SEL = 16
WINDOW = 512
Q_BLOCK = 128
D_FF = 2816
EPS = 1e-6
NEG = -1e30
FORCE = 1e4

LANES = 128
SUBLANES = 8
VMEM_LIMIT = 56 * 1024 * 1024

GLA_SUB = 16
SEL_TILE = 1024
CMP_PAGES = 32

_P_A, _P_QK, _P_V, _P_O, _P_LR, _P_UP, _P_NQ, _P_KV, _P_NG, _P_END = (
    0, 512, 768, 1024, 1280, 1408, 1664, 1920, 2688, 2816)


def _bdot(a, b):
    return jnp.dot(a.astype(BF16), b.astype(BF16), preferred_element_type=F32)


def _nt(a, b):
    return lax.dot_general(a, b, (((1,), (1,)), ((), ())), preferred_element_type=F32)


def _tn(a, b):
    return lax.dot_general(a, b, (((0,), (0,)), ((), ())), preferred_element_type=F32)


def _split(x):
    hi = x.astype(BF16)
    return hi, (x - hi.astype(F32)).astype(BF16)


def _rms(x, g):
    return x * lax.rsqrt(jnp.mean(x * x, axis=-1, keepdims=True) + EPS) * g


def _segmean(x2, e_ref):
    hi, lo = _split(x2)
    e = e_ref[...]
    return jnp.dot(hi, e, preferred_element_type=F32) + jnp.dot(lo, e, preferred_element_type=F32)


def _const_spec(shape):
    nd = len(shape)
    return pl.BlockSpec(shape, lambda *_: (0,) * nd)


def _params(sem):
    return pltpu.CompilerParams(dimension_semantics=sem, vmem_limit_bytes=VMEM_LIMIT)


def _proj_kernel(x_ref, g1_ref, wp_ref, wa2_ref, ba_ref, qn_ref, kn1_ref, kn2_ref, gb_ref, e256_ref, e128_ref,
                 a_ref, qk_ref, v_ref, o_ref, la_ref, up_ref, q_ref, rows_ref, win_ref, ng_ref, *t_refs):
    h = _rms(x_ref[...], g1_ref[...]).astype(BF16)

    def seg(a, b):
        return _nt(h, wp_ref[a:b, :])

    a_ref[...] = seg(_P_A, _P_QK)
    qk_ref[...] = seg(_P_QK, _P_V)
    v_ref[...] = seg(_P_V, _P_O)
    o_ref[...] = seg(_P_O, _P_LR)
    z = _bdot(seg(_P_LR, _P_UP), wa2_ref[...]) + ba_ref[...]
    la_ref[...] = (jnp.minimum(z, 0.0) - jnp.log(1.0 + jnp.exp(-jnp.abs(z)))) * (1.0 / GLA_TAU)
    up_ref[...] = seg(_P_UP, _P_NQ)
    qr = seg(_P_NQ, _P_KV)
    q_ref[...] = qr * lax.rsqrt(_segmean(qr * qr, e256_ref) + EPS) * qn_ref[...]
    kv = seg(_P_KV, _P_NG)
    k_sel = kv[:, 256:384]
    k_sel = k_sel * lax.rsqrt(_segmean(k_sel * k_sel, e128_ref) + EPS) * kn1_ref[...]
    k_win = kv[:, 512:640]
    k_win = k_win * lax.rsqrt(_segmean(k_win * k_win, e128_ref) + EPS) * kn2_ref[...]
    v_sel, v_win = kv[:, 384:512], kv[:, 640:768]
    rows_ref[:, 0:256] = kv[:, 0:256]
    rows_ref[:, 256:384] = k_sel
    rows_ref[:, 384:512] = v_sel
    win_ref[:, 0:128] = k_win
    win_ref[:, 128:256] = v_win
    ng_ref[...] = jax.nn.sigmoid(seg(_P_NG, _P_END) + gb_ref[...])
    if t_refs:
        ksb_ref, svt_ref, kwb_ref, wvt_ref = t_refs
        ksb_ref[...] = k_sel.astype(BF16)
        svt_ref[...] = v_sel.T.astype(BF16)
        kwb_ref[...] = k_win.astype(BF16)
        wvt_ref[...] = v_win.T.astype(BF16)


def _proj_call(x, lw, with_t):
    n = x.shape[0]
    tm = min(512, n)
    row = lambda w: pl.BlockSpec((tm, w), lambda i: (i, 0))
    consts = (lw['g1'], lw['wp'], lw['wa2'], lw['ba'], lw['qn'], lw['kn1'], lw['kn2'], lw['gb'], lw['e256'], lw['e128'])
    widths = (512, 256, 256, 256, 128, 256, 256, 512, 256, 128)
    out_shape = [jax.ShapeDtypeStruct((n, w), F32) for w in widths]
    out_specs = [row(w) for w in widths]
    if with_t:
        col = pl.BlockSpec((LANES, tm), lambda i: (0, i))
        out_shape += [jax.ShapeDtypeStruct((n, LANES), BF16), jax.ShapeDtypeStruct((LANES, n), BF16)] * 2
        out_specs += [row(LANES), col] * 2
    return pl.pallas_call(
        _proj_kernel, grid=(n // tm,),
        in_specs=[row(D_MODEL)] + [_const_spec(c.shape) for c in consts],
        out_specs=out_specs, out_shape=out_shape,
        compiler_params=_params(("parallel",)), name="proj",
    )(x, *consts)


_CONV_PAD = 32


def _conv_kernel(a_ref, prev_ref, w_ref, b_ref, lg_ref, lb_ref, y_ref, u_ref, ext_ref, z_ref, *, tm):
    zrows = z_ref.shape[0]

    @pl.when(pl.program_id(1) == 0)
    def _():
        ext_ref[0:_CONV_PAD, :] = prev_ref[0]
        tail = ext_ref.shape[0] - (_CONV_PAD + tm)
        ext_ref[_CONV_PAD + tm:, :] = jnp.zeros((tail, CONV_CH), F32)

    a = a_ref[0]
    u = a[:, :CONV_CH] * jax.nn.sigmoid(a[:, CONV_CH:])
    ext_ref[_CONV_PAD:_CONV_PAD + tm, :] = u
    acc = jnp.zeros((tm, CONV_CH), F32) + b_ref[...]
    first = _CONV_PAD - (CONV_WIDTH - 1)
    for b in range(SUBLANES):
        z = None
        for j in range(b, CONV_WIDTH, SUBLANES):
            term = ext_ref[j - b:j - b + zrows, :] * w_ref[j:j + 1, :]
            z = term if z is None else z + term
        z_ref[...] = z
        acc = acc + z_ref[first + b:first + b + tm, :]
    mu = jnp.mean(acc, axis=-1, keepdims=True)
    cen = acc - mu
    var = jnp.mean(cen * cen, axis=-1, keepdims=True)
    yn = cen * lax.rsqrt(var + EPS) * lg_ref[...] + lb_ref[...]
    y_ref[0] = yn * jax.nn.sigmoid(yn)
    u_ref[0] = u
    ext_ref[0:_CONV_PAD, :] = ext_ref[tm:tm + _CONV_PAD, :]


def _conv_call(a, prev, lw):
    b, t, _ = a.shape
    tm = min(512, t)
    prev = jnp.pad(prev, ((0, 0), (_CONV_PAD - (CONV_WIDTH - 1), 0), (0, 0)))
    blk = lambda w: pl.BlockSpec((1, tm, w), lambda i, j: (i, j, 0))
    consts = (lw['cw'], lw['cb'], lw['clg'], lw['clb'])
    return pl.pallas_call(
        functools.partial(_conv_kernel, tm=tm), grid=(b, t // tm),
        in_specs=[blk(2 * CONV_CH), pl.BlockSpec((1, _CONV_PAD, CONV_CH), lambda i, j: (i, 0, 0))]
        + [_const_spec(c.shape) for c in consts],
        out_specs=[blk(CONV_CH), blk(CONV_CH)],
        out_shape=[jax.ShapeDtypeStruct((b, t, CONV_CH), F32)] * 2,
        scratch_shapes=[pltpu.VMEM((_CONV_PAD + tm + 2 * SUBLANES, CONV_CH), F32),
                        pltpu.VMEM((tm + 2 * SUBLANES, CONV_CH), F32)],
        compiler_params=_params(("arbitrary", "arbitrary")), name="conv",
    )(a, prev, *consts)


_POOL_PAD = 16


def _pool_kernel(u_ref, prev_ref, w_ref, sc_ref, y_ref, ext_ref, *, tm, pos0):
    t = pl.program_id(1)

    @pl.when(t == 0)
    def _():
        ext_ref[0:_POOL_PAD, :] = prev_ref[0]

    u = u_ref[0]
    ext_ref[_POOL_PAD:_POOL_PAD + tm, :] = u

    def back(d):
        return ext_ref[_POOL_PAD - d:_POOL_PAD - d + tm, :]

    sums, acc, d = [], u, 1
    for w in POOL_WINDOWS:
        while d < w:
            acc = acc + back(d)
            d += 1
        sums.append(acc)
    grp = lax.broadcasted_iota(jnp.int32, (tm, POOL_CH), 1) // (POOL_CH // len(POOL_WINDOWS))
    pos = pos0 + t * tm + lax.broadcasted_iota(jnp.int32, (tm, POOL_CH), 0)
    win, width = sums[-1], jnp.full((tm, POOL_CH), POOL_WINDOWS[-1], jnp.int32)
    for gi in range(len(POOL_WINDOWS) - 2, -1, -1):
        win = jnp.where(grp == gi, sums[gi], win)
        width = jnp.where(grp == gi, POOL_WINDOWS[gi], width)
    cnt = jnp.minimum(pos + 1, width).astype(F32)
    pooled = win / cnt - u
    y_ref[0] = _bdot(pooled, w_ref[...]) * sc_ref[...]
    ext_ref[0:_POOL_PAD, :] = ext_ref[tm:tm + _POOL_PAD, :]


def _pool_call(u, prev, pos0, lw):
    b, t, _ = u.shape
    tm = min(512, t)
    prev = jnp.pad(prev, ((0, 0), (_POOL_PAD - POOL_STATE, 0), (0, 0)))
    blk = pl.BlockSpec((1, tm, POOL_CH), lambda i, j: (i, j, 0))
    return pl.pallas_call(
        functools.partial(_pool_kernel, tm=tm, pos0=pos0), grid=(b, t // tm),
        in_specs=[blk, pl.BlockSpec((1, _POOL_PAD, POOL_CH), lambda i, j: (i, 0, 0)),
                  _const_spec(lw['pw'].shape), _const_spec(lw['psc'].shape)],
        out_specs=blk, out_shape=jax.ShapeDtypeStruct((b, t, POOL_CH), F32),
        scratch_shapes=[pltpu.VMEM((_POOL_PAD + tm, POOL_CH), F32)],
        compiler_params=_params(("arbitrary", "arbitrary")), name="pool",
    )(u, prev, lw['pw'], lw['psc'])


def _gla_kernel(qk_ref, v_ref, la_ref, og_ref, s0_ref, on_ref, eh_ref, mbd_ref, e256_ref,
                y_ref, sfin_ref, st_ref, kext, bext, vext, qs_s, ks_s, g_s, o_s, *, tm):
    R = GLA_SUB
    t = pl.program_id(1)
    kw = GLA_HEADS * GLA_DK

    @pl.when(t == 0)
    def _():
        st_ref[...] = s0_ref[0]
        kext[0:R, :] = jnp.zeros((R, kw), F32)
        bext[0:R, :] = jnp.zeros((R, kw), F32)
        vext[0:R, :] = jnp.zeros((R, GLA_HEADS * GLA_DV), F32)

    qk = qk_ref[0]
    q = qk[:, :kw] * (GLA_DK ** -0.5)
    k = qk[:, kw:]
    v = v_ref[0]
    la = la_ref[0]
    r = lax.broadcasted_iota(jnp.int32, (tm, kw), 0) % R
    b = la
    c = la
    for s in (1, 2, 4, 8):
        b = b + jnp.where(r >= s, pltpu.roll(b, s, axis=0), 0.0)
        c = c + jnp.where(r < R - s, pltpu.roll(c, tm - s, axis=0), 0.0)
    kext[R:R + tm, :] = k
    bext[R:R + tm, :] = b
    vext[R:R + tm, :] = v
    o = jnp.zeros((tm, GLA_HEADS * GLA_DV), F32)
    for d in range(R):
        kd = kext[R - d:R - d + tm, :]
        bd = bext[R - d:R - d + tm, :]
        vd = vext[R - d:R - d + tm, :]
        e = jnp.exp(jnp.where(r >= d, b - bd, NEG))
        a = jnp.dot((q * kd * e).astype(BF16), eh_ref[...], preferred_element_type=F32)
        o = o + a * vd
    o_s[...] = o
    qs_s[...] = q * jnp.exp(b)
    ks_s[...] = k * jnp.exp(c - la)
    g_s[...] = jnp.exp(b + c - la)

    nsb = tm // R
    group = 4 if nsb % 4 == 0 else 1

    def body(i, carry):
        st = st_ref[...]
        for u in range(group):
            r0 = pl.multiple_of((i * group + u) * R, R)
            o_s[pl.ds(r0, R), :] += _nt(qs_s[pl.ds(r0, R), :].astype(BF16), st.astype(BF16))
            upd = _tn(vext[pl.ds(R + r0, R), :].astype(BF16), ks_s[pl.ds(r0, R), :].astype(BF16))
            st = st * g_s[pl.ds(r0, 1), :] + upd * mbd_ref[...]
        st_ref[...] = st
        return carry

    lax.fori_loop(0, nsb // group, body, 0)
    o = o_s[...]
    on = o * lax.rsqrt(_segmean(o * o, e256_ref) + EPS) * on_ref[...]
    og = og_ref[0]
    y_ref[0] = on * (og * jax.nn.sigmoid(og))

    @pl.when(t == pl.num_programs(1) - 1)
    def _():
        sfin_ref[0] = st_ref[...]


def _gla_call(qk, v, la, og, s0, lw):
    b, t, _ = qk.shape
    tp = -(-t // GLA_SUB) * GLA_SUB
    if tp != t:
        pad = lambda a: jnp.pad(a, ((0, 0), (0, tp - t), (0, 0)))
        qk, v, la, og = pad(qk), pad(v), pad(la), pad(og)
    tm = min(256, tp)
    kw, vw = GLA_HEADS * GLA_DK, GLA_HEADS * GLA_DV
    eye = jnp.eye(GLA_HEADS, dtype=F32)
    st0 = jnp.einsum('bhkv,hg->bhvgk', s0.astype(F32), eye).reshape(b, vw, kw)
    blk = lambda w: pl.BlockSpec((1, tm, w), lambda i, j: (i, j, 0))
    st_spec = pl.BlockSpec((1, vw, kw), lambda i, j: (i, 0, 0))
    consts = (lw['on'], lw['eh'], lw['mbd'], lw['e256'])
    y, st = pl.pallas_call(
        functools.partial(_gla_kernel, tm=tm), grid=(b, tp // tm),
        in_specs=[blk(2 * kw), blk(vw), blk(kw), blk(vw), st_spec] + [_const_spec(c.shape) for c in consts],
        out_specs=[blk(vw), st_spec],
        out_shape=[jax.ShapeDtypeStruct((b, tp, vw), F32), jax.ShapeDtypeStruct((b, vw, kw), F32)],
        scratch_shapes=[pltpu.VMEM((vw, kw), F32),
                        pltpu.VMEM((GLA_SUB + tm, kw), F32), pltpu.VMEM((GLA_SUB + tm, kw), F32),
                        pltpu.VMEM((GLA_SUB + tm, vw), F32),
                        pltpu.VMEM((tm, kw), F32), pltpu.VMEM((tm, kw), F32), pltpu.VMEM((tm, kw), F32),
                        pltpu.VMEM((tm, vw), F32)],
        compiler_params=_params(("arbitrary", "arbitrary")), name="gla",
    )(qk, v, la, og, st0, *consts)
    st = st.reshape(b, GLA_HEADS, GLA_DV, GLA_HEADS, GLA_DK)
    s_fin = jnp.stack([st[:, h, :, h, :] for h in range(GLA_HEADS)], axis=1).swapaxes(-1, -2)
    return y[:, :t], s_fin


def _cmp_copy(pt_ref, cache_ref, buf, sem, step, slot, j, kind, n_steps, page, feature_major):
    bb, ss = step // n_steps, step % n_steps
    pg = pt_ref[bb, ss * (buf.shape[2] // page) + j]
    src = (cache_ref.at[pg, pl.ds(kind * LANES, LANES), :] if feature_major
           else cache_ref.at[pg, :, pl.ds(kind * LANES, LANES)])
    return pltpu.make_async_copy(src, buf.at[slot, kind, pl.ds(j * page, page), :], sem.at[slot])


def _compress_kernel(pt_ref, cache_ref, wk_ref, wv_ref, pek_ref, pev_ref, kn0_ref, e128_ref,
                     kch_ref, kcl_ref, vct_ref, buf, sem, f_s, bias_s, xs, *, n_steps, page, total, feature_major):
    copy = functools.partial(_cmp_copy, pt_ref, cache_ref, buf, sem, n_steps=n_steps, page=page,
                             feature_major=feature_major)
    bi, si = pl.program_id(0), pl.program_id(1)
    step = bi * n_steps + si
    slot = step % 2
    rows = buf.shape[2]
    pages_per_step = rows // page
    nchunk = rows // CMP_STRIDE

    def start(st, sl):
        for j in range(pages_per_step):
            for kind in range(2):
                copy(st, sl, j, kind).start()

    @pl.when(step == 0)
    def _():
        start(step, slot)
        rid = lax.broadcasted_iota(jnp.int32, (CMP_STRIDE, 2 * LANES), 0)
        bk = jnp.zeros((CMP_STRIDE, 2 * LANES), F32)
        bv = jnp.zeros((CMP_STRIDE, 2 * LANES), F32)
        for l in range(CMP_STRIDE):
            pk, pv = pek_ref[...], pev_ref[...]
            w_k = wk_ref[l // 2][(l % 2) * LANES:(l % 2 + 1) * LANES, :]
            w_v = wv_ref[l // 2][(l % 2) * LANES:(l % 2 + 1) * LANES, :]
            tk = jnp.concatenate([_bdot(pk[:, :LANES], w_k[:, :LANES]), _bdot(pk[:, LANES:], w_k[:, LANES:])], axis=1)
            tv = jnp.concatenate([_bdot(pv[:, :LANES], w_v[:, :LANES]), _bdot(pv[:, LANES:], w_v[:, LANES:])], axis=1)
            bk = bk + jnp.where(rid == l, tk, 0.0)
            bv = bv + jnp.where(rid == l, tv, 0.0)
        bias_s[0:1, :] = jnp.sum(bk, axis=0, keepdims=True)
        bias_s[1:2, :] = jnp.sum(bv, axis=0, keepdims=True)

    @pl.when(step + 1 < total)
    def _():
        start(step + 1, 1 - slot)

    for j in range(pages_per_step):
        for kind in range(2):
            copy(step, slot, j, kind).wait()
    pitch = xs.shape[1] // CMP_STRIDE
    if feature_major:
        for j in range(pages_per_step):
            for kind in range(2):
                x = buf[slot, kind, j * page:(j + 1) * page, :].T
                for v in range(page // SUBLANES):
                    m, l0 = j * (page // CMP_STRIDE) + (v * SUBLANES) // CMP_STRIDE, (v * SUBLANES) % CMP_STRIDE
                    xs[kind, pl.ds(l0 * pitch + m, SUBLANES, stride=pitch), :] = x[v * SUBLANES:(v + 1) * SUBLANES, :]

    def rows_of(kind, l):
        if feature_major:
            return xs[kind, l * pitch:l * pitch + nchunk, :].astype(BF16)
        return buf[slot, kind, pl.ds(l, nchunk, stride=CMP_STRIDE), :].astype(BF16)

    fk = jnp.zeros((nchunk, 2 * LANES), F32)
    fv = jnp.zeros((nchunk, 2 * LANES), F32)
    for l in range(0, CMP_STRIDE, 2):
        xk = jnp.concatenate([rows_of(0, l), rows_of(0, l + 1)], axis=1)
        xv = jnp.concatenate([rows_of(1, l), rows_of(1, l + 1)], axis=1)
        fk = fk + jnp.dot(xk, wk_ref[l // 2], preferred_element_type=F32)
        fv = fv + jnp.dot(xv, wv_ref[l // 2], preferred_element_type=F32)
    m0 = pl.multiple_of(si * nchunk, nchunk)
    f_s[pl.ds(m0, nchunk), 0:2 * LANES] = fk
    f_s[pl.ds(m0, nchunk), 2 * LANES:4 * LANES] = fv

    @pl.when(si == n_steps - 1)
    def _():
        ncp = f_s.shape[0]
        last = lax.broadcasted_iota(jnp.int32, (ncp, LANES), 0) == ncp - 1
        kc = f_s[:, 0:LANES] + pltpu.roll(f_s[:, LANES:2 * LANES], ncp - 1, axis=0)
        kc = kc + bias_s[0:1, 0:LANES] + bias_s[0:1, LANES:2 * LANES]
        kc = jnp.where(last, 0.0, kc)
        kc = kc * lax.rsqrt(_segmean(kc * kc, e128_ref) + EPS) * kn0_ref[...]
        hi, lo = _split(kc)
        kch_ref[0] = hi
        kcl_ref[0] = lo
        vc = f_s[:, 2 * LANES:3 * LANES] + pltpu.roll(f_s[:, 3 * LANES:4 * LANES], ncp - 1, axis=0)
        vc = vc + bias_s[1:2, 0:LANES] + bias_s[1:2, LANES:2 * LANES]
        vct_ref[0] = jnp.where(last, 0.0, vc).T.astype(BF16)


def _compress_call(cache, page_table, lw, feature_major):
    b, n_pages = page_table.shape
    page = cache.shape[2] if feature_major else cache.shape[1]
    assert page % CMP_STRIDE == 0 and (page == LANES or not feature_major)
    pages_per_step = max(d for d in range(1, CMP_PAGES + 1) if n_pages % d == 0)
    n_steps = n_pages // pages_per_step
    ncp = n_pages * page // CMP_STRIDE
    rows = pages_per_step * page
    consts = (lw['wk'], lw['wv'], lw['pek'], lw['pev'], lw['kn0'], lw['e128'])
    out3 = lambda s: pl.BlockSpec((1,) + s, lambda i, j, pt: (i, 0, 0))
    grid_spec = pltpu.PrefetchScalarGridSpec(
        num_scalar_prefetch=1, grid=(b, n_steps),
        in_specs=[pl.BlockSpec(memory_space=pl.ANY)]
        + [pl.BlockSpec(c.shape, lambda i, j, pt, nd=c.ndim: (0,) * nd) for c in consts],
        out_specs=[out3((ncp, LANES)), out3((ncp, LANES)), out3((LANES, ncp))],
        scratch_shapes=[pltpu.VMEM((2, 2, rows, LANES), F32), pltpu.SemaphoreType.DMA((2,)),
                        pltpu.VMEM((ncp, 4 * LANES), F32), pltpu.VMEM((SUBLANES, 2 * LANES), F32),
                        pltpu.VMEM((2, CMP_STRIDE * (rows // CMP_STRIDE + SUBLANES), LANES), F32)])
    return pl.pallas_call(
        functools.partial(_compress_kernel, n_steps=n_steps, page=page, total=b * n_steps,
                          feature_major=feature_major),
        grid_spec=grid_spec,
        out_shape=[jax.ShapeDtypeStruct((b, ncp, LANES), BF16), jax.ShapeDtypeStruct((b, ncp, LANES), BF16),
                   jax.ShapeDtypeStruct((b, LANES, ncp), BF16)],
        compiler_params=_params(("arbitrary", "arbitrary")), name="compress",
    )(page_table, cache, *consts)


def _softmax_cols(s, mask):
    sm = jnp.where(mask, s, NEG)
    m = jnp.max(sm, axis=0, keepdims=True)
    p = jnp.where(mask, jnp.exp(sm - m), 0.0)
    return p, m, jnp.sum(p, axis=0, keepdims=True)


def _flash_update(carry, s, mask, vt):
    m, l, acc = carry
    sm = jnp.where(mask, s, NEG)
    m_new = jnp.maximum(m, jnp.max(sm, axis=0, keepdims=True))
    alpha = jnp.exp(m - m_new)
    p = jnp.where(mask, jnp.exp(sm - m_new), 0.0)
    l = alpha * l + jnp.sum(p, axis=0, keepdims=True)
    acc = acc * alpha + jnp.dot(vt, p.astype(BF16), preferred_element_type=F32)
    return m_new, l, acc


def _compressed_branch(kch, kcl, vct, qp, colpos):
    qh, ql = _split(qp)
    s = _nt(kch, qh) + _nt(kch, ql) + _nt(kcl, qh)
    n_idx = lax.broadcasted_iota(jnp.int32, s.shape, 0)
    mask = n_idx <= jnp.right_shift(colpos - (2 * CMP_STRIDE - 1), CMP_STRIDE.bit_length() - 1)
    p, _, l = _softmax_cols(s, mask)
    p = p / jnp.maximum(l, 1e-30)
    return p, jnp.dot(vct, p.astype(BF16), preferred_element_type=F32)


def _block_scores(mbt, imp, colpos):
    ih, il = _split(imp)
    blk = jnp.dot(mbt, ih, preferred_element_type=F32) + jnp.dot(mbt, il, preferred_element_type=F32)
    j_idx = lax.broadcasted_iota(jnp.int32, blk.shape, 0)
    avail = j_idx * SEL_BLOCK <= colpos
    forced = (j_idx == colpos // SEL_BLOCK) | (j_idx == 0)
    return jnp.where(avail, jnp.where(forced, FORCE, blk), NEG)


def _pick_round(x, j_f):
    m = jnp.max(x, axis=0, keepdims=True)
    first = jnp.min(jnp.where(x == m, j_f, 1e9), axis=0, keepdims=True)
    pick = j_f == first
    return m, first, pick, jnp.where(pick, -jnp.inf, x)


def _head_rows_to_lanes(o_rows, gates, rows_per_head):
    n = rows_per_head
    lo_half = lax.broadcasted_iota(jnp.int32, (n, LANES), 1) < NSA_HD
    outs = []
    for h in range(NSA_HEADS):
        acc = None
        for c, o in enumerate(o_rows):
            term = gates[:, 3 * h + c:3 * h + c + 1] * o[h * n:(h + 1) * n, :]
            acc = term if acc is None else acc + term
        outs.append(acc)
    y01 = jnp.where(lo_half, outs[0], pltpu.roll(outs[1], NSA_HD, axis=1))
    y23 = jnp.where(lo_half, pltpu.roll(outs[2], NSA_HD, axis=1), outs[3])
    return jnp.concatenate([y01, y23], axis=1)


def _nsa_prompt_kernel(q_ref, ng_ref, kch_ref, kcl_ref, vct_ref, mbt_ref, ks_ref, svt_ref, kw_ref, wvt_ref,
                       e16_ref, y_ref, sel_s, oc_s, *, n_variants):
    qb_rows = Q_BLOCK
    s0 = pl.program_id(1) * qb_rows
    q = q_ref[0]
    lo_half = lax.broadcasted_iota(jnp.int32, (qb_rows, LANES), 1) < NSA_HD
    q01, q23 = q[:, :LANES], q[:, LANES:]
    qp = jnp.concatenate([jnp.where(lo_half, q01, 0.0),
                          jnp.where(lo_half, pltpu.roll(q01, NSA_HD, axis=1), 0.0),
                          jnp.where(lo_half, 0.0, pltpu.roll(q23, NSA_HD, axis=1)),
                          jnp.where(lo_half, 0.0, q23)], axis=0)
    ncols = NSA_HEADS * qb_rows
    colpos = s0 + lax.broadcasted_iota(jnp.int32, (1, ncols), 1) % qb_rows
    qb = qp.astype(BF16)

    ncp, nbp = kch_ref.shape[1], sel_s.shape[0]

    def select(nc, nb):
        p, oc_t = _compressed_branch(kch_ref[0, :nc, :], kcl_ref[0, :nc, :], vct_ref[0, :, :nc], qp, colpos)
        oc_s[...] = oc_t
        imp = jnp.concatenate([p[:, 0:qb_rows] + p[:, qb_rows:2 * qb_rows],
                               p[:, 2 * qb_rows:3 * qb_rows] + p[:, 3 * qb_rows:]], axis=1)
        x0 = _block_scores(mbt_ref[:nb, :nc], imp, colpos[:, :2 * qb_rows])
        j_f = lax.broadcasted_iota(jnp.int32, x0.shape, 0).astype(F32)
        x = x0
        for _ in range(N_SEL):
            _, _, _, x = _pick_round(x, j_f)
        sel = jnp.where((x == -jnp.inf) & (x0 > 0.5 * NEG), 0.0, NEG)
        sel_s[0:nb, :] = jnp.concatenate([sel[:, :qb_rows], sel[:, :qb_rows], sel[:, qb_rows:], sel[:, qb_rows:]], axis=1)

    variant = pl.program_id(1) // (pl.num_programs(1) // n_variants)
    for v in range(n_variants):
        pl.when(variant == v)(functools.partial(select, ncp * (v + 1) // n_variants, nbp * (v + 1) // n_variants))
    oc_t = oc_s[...]

    kt = SEL_TILE
    nblk = kt // SEL_BLOCK
    qpt = qp.T.astype(BF16)
    pad_rows = jnp.zeros((LANES - nblk, ncols), BF16)

    def queries(c):
        bias = sel_s[pl.ds(pl.multiple_of(c * nblk, nblk), nblk), :].astype(BF16)
        return jnp.concatenate([qpt, bias, pad_rows], axis=0)

    def scores(c):
        k0 = pl.multiple_of(c * kt, kt)
        keys = jnp.concatenate([ks_ref[0, pl.ds(k0, kt), :], e16_ref[...]], axis=1)
        return k0, jnp.dot(keys, queries(c), preferred_element_type=F32)

    def unmasked(carry, k0, s):
        m, l, acc = carry
        m_new = jnp.maximum(m, jnp.max(s, axis=0, keepdims=True))
        alpha = jnp.exp(m - m_new)
        p = jnp.exp(s - m_new)
        l = alpha * l + jnp.sum(p, axis=0, keepdims=True)
        acc = acc * alpha + jnp.dot(svt_ref[0, :, pl.ds(k0, kt)], p.astype(BF16), preferred_element_type=F32)
        return m_new, l, acc

    def pair(c, carry):
        ka, sa = scores(2 * c)
        kb, sb = scores(2 * c + 1)
        return unmasked(unmasked(carry, ka, sa), kb, sb)

    def single(c, carry):
        return unmasked(carry, *scores(c))

    init = (jnp.full((1, ncols), NEG, F32), jnp.zeros((1, ncols), F32), jnp.zeros((LANES, ncols), F32))
    last = (s0 + qb_rows - 1) // kt
    carry = lax.fori_loop(0, last // 2, pair, init)
    carry = lax.fori_loop(2 * (last // 2), last, single, carry)

    sub = 4 * Q_BLOCK
    k_diag = pl.multiple_of(last * kt, kt)
    q_diag = queries(last)

    def diagonal(j, carry):
        off = pl.multiple_of(j * sub, sub)
        k0 = pl.multiple_of(k_diag + off, sub)
        keys = jnp.concatenate([ks_ref[0, pl.ds(k0, sub), :], e16_ref[pl.ds(off, sub), :]], axis=1)
        s = jnp.dot(keys, q_diag, preferred_element_type=F32)
        visible = lax.broadcasted_iota(jnp.int32, s.shape, 0) <= colpos - k0
        return _flash_update(carry, s, visible, svt_ref[0, :, pl.ds(k0, sub)])

    _, l, acc = lax.fori_loop(0, (s0 + qb_rows - k_diag + sub - 1) // sub, diagonal, carry)
    os_t = acc / jnp.maximum(l, 1e-30)

    span = WINDOW + qb_rows
    w0 = pl.multiple_of(jnp.maximum(s0 - WINDOW, 0), qb_rows)
    s = _nt(kw_ref[0, pl.ds(w0, span), :], qb)
    krow = lax.broadcasted_iota(jnp.int32, s.shape, 0)
    newest = colpos - w0
    pw, _, lw_ = _softmax_cols(s, (krow <= newest) & (krow > newest - WINDOW))
    ow_t = jnp.dot(wvt_ref[0, :, pl.ds(w0, span)], pw.astype(BF16), preferred_element_type=F32) / jnp.maximum(lw_, 1e-30)

    y_ref[0] = _head_rows_to_lanes([oc_t.T, os_t.T, ow_t.T], ng_ref[0], qb_rows)


def _nsa_prompt_call(q, ng, kch, kcl, vct, ksb, svt, kwb, wvt, lw):
    b, t, _ = q.shape
    assert t % SEL_TILE == 0 and t >= WINDOW + Q_BLOCK
    ncp = kch.shape[1]
    nbp = t // SEL_BLOCK
    n_variants = max(v for v in (1, 2, 4, 8) if (t // v) % SEL_TILE == 0 and (ncp // v) % LANES == 0)
    full = lambda s: pl.BlockSpec((1,) + s, lambda i, j: (i, 0, 0))
    blk = lambda w: pl.BlockSpec((1, Q_BLOCK, w), lambda i, j: (i, j, 0))
    return pl.pallas_call(
        functools.partial(_nsa_prompt_kernel, n_variants=n_variants), grid=(b, t // Q_BLOCK),
        in_specs=[blk(2 * LANES), blk(LANES), full((ncp, LANES)), full((ncp, LANES)), full((LANES, ncp)),
                  _const_spec(lw['mbt_p'].shape), full((t, LANES)), full((LANES, t)), full((t, LANES)),
                  full((LANES, t)), _const_spec(lw['e16'].shape)],
        out_specs=blk(2 * LANES), out_shape=jax.ShapeDtypeStruct((b, t, 2 * LANES), F32),
        scratch_shapes=[pltpu.VMEM((nbp, NSA_HEADS * Q_BLOCK), F32), pltpu.VMEM((LANES, NSA_HEADS * Q_BLOCK), F32)],
        compiler_params=_params(("arbitrary", "arbitrary")), name="nsa_prompt",
    )(q, ng, kch, kcl, vct, lw['mbt_p'], ksb, svt, kwb, wvt, lw['e16'])


_SCOL = SUBLANES


def _nsa_sample_a_kernel(qp_ref, kch_ref, kcl_ref, vct_ref, mbt_ref, cwin_ref, wnew_ref,
                         oc_ref, ow_ref, idx_ref, ok_ref, *, past_len, n_dec):
    qp = qp_ref[0]
    qb = qp.astype(BF16)
    col = lax.broadcasted_iota(jnp.int32, (1, LANES), 1)
    tcol = col % _SCOL
    colpos = past_len + tcol
    p, oc_t = _compressed_branch(kch_ref[0], kcl_ref[0], vct_ref[0], qp, colpos)
    oc_ref[0] = oc_t
    first_head = (col % (2 * _SCOL)) < _SCOL
    imp = p + jnp.where(first_head, pltpu.roll(p, LANES - _SCOL, axis=1), pltpu.roll(p, _SCOL, axis=1))
    x = _block_scores(mbt_ref[...], imp, colpos)
    j_f = lax.broadcasted_iota(jnp.int32, x.shape, 0).astype(F32)
    for rd in range(N_SEL - 1):
        m, first, _, x = _pick_round(x, j_f)
        idx_ref[0, rd:rd + 1, :] = first.astype(jnp.int32)
        ok_ref[0, rd:rd + 1, :] = jnp.where(m > 0.5 * NEG, 1.0, 0.0)
    idx_ref[0, N_SEL - 1:N_SEL, :] = jnp.zeros((1, LANES), jnp.int32)
    ok_ref[0, N_SEL - 1:N_SEL, :] = jnp.ones((1, LANES), F32)

    kp, kn = cwin_ref[0], wnew_ref[0]
    wbuf = kp.shape[0]
    s1 = _nt(kp[:, :LANES].astype(BF16), qb)
    kpos1 = past_len - wbuf + lax.broadcasted_iota(jnp.int32, s1.shape, 0)
    d1 = colpos - kpos1
    mask1 = (d1 >= 0) & (d1 < WINDOW) & (kpos1 >= 0)
    s2 = _nt(kn[:, :LANES].astype(BF16), qb)
    kidx = lax.broadcasted_iota(jnp.int32, s2.shape, 0)
    mask2 = (kidx <= tcol) & (kidx < n_dec)
    carry = (jnp.full((1, LANES), NEG, F32), jnp.zeros((1, LANES), F32), jnp.zeros((LANES, LANES), F32))
    carry = _flash_update(carry, s1, mask1, kp[:, LANES:].T.astype(BF16))
    _, l, acc = _flash_update(carry, s2, mask2, kn[:, LANES:].T.astype(BF16))
    ow_ref[0] = acc / jnp.maximum(l, 1e-30)


def _nsa_sample_a_call(qp, kch, kcl, vct, cwin, wnew, lw, past_len, n_dec):
    b = qp.shape[0]
    ncp = kch.shape[1]
    full = lambda a: pl.BlockSpec((1,) + a.shape[1:], lambda i: (i, 0, 0))
    sq = pl.BlockSpec((1, LANES, LANES), lambda i: (i, 0, 0))
    rnd = pl.BlockSpec((1, N_SEL, LANES), lambda i: (i, 0, 0))
    return pl.pallas_call(
        functools.partial(_nsa_sample_a_kernel, past_len=past_len, n_dec=n_dec), grid=(b,),
        in_specs=[sq, full(kch), full(kcl), full(vct), _const_spec(lw['mbt_s'].shape), full(cwin), full(wnew)],
        out_specs=[sq, sq, rnd, rnd],
        out_shape=[jax.ShapeDtypeStruct((b, LANES, LANES), F32)] * 2
        + [jax.ShapeDtypeStruct((b, N_SEL, LANES), jnp.int32), jax.ShapeDtypeStruct((b, N_SEL, LANES), F32)],
        compiler_params=_params(("parallel",)), name="nsa_sample_a",
    )(qp, kch, kcl, vct, lw['mbt_s'], cwin, wnew)


def _sel_copy(pt_ref, info_ref, cache_ref, kvbuf, sem, bi, slot, c, rd, n_dec, page):
    j = info_ref[bi, c * (N_SEL - 1) + rd]
    pg = pt_ref[bi, j // (page // SEL_BLOCK)]
    return pltpu.make_async_copy(cache_ref.at[pg, pl.ds(2, 2), c // n_dec],
                                 kvbuf.at[slot, c, :, :, pl.ds(rd * page, page)], sem.at[slot])


def _nsa_sample_b_kernel(pt_ref, info_ref, cache_ref, qpt_ref, newt_ref, oc_ref, ow_ref, ngt_ref,
                         y_ref, kvbuf, sem, *, n_dec, page):
    bi = pl.program_id(0)
    slot = bi % 2
    ncomb = NSA_KV * n_dec
    nsel = N_SEL - 1
    past_keys = nsel * page
    copy = functools.partial(_sel_copy, pt_ref, info_ref, cache_ref, kvbuf, sem, n_dec=n_dec, page=page)

    def start(b, sl):
        for c in range(ncomb):
            for rd in range(nsel):
                copy(b, sl, c, rd).start()

    @pl.when(bi == 0)
    def _():
        start(bi, slot)

    @pl.when(bi + 1 < pl.num_programs(0))
    def _():
        start(bi + 1, 1 - slot)

    newt = newt_ref[0]
    for c in range(ncomb):
        g = c // n_dec
        kvbuf[slot, c, 0, :, past_keys:past_keys + LANES] = newt[g * NSA_HD:(g + 1) * NSA_HD, :]
        kvbuf[slot, c, 1, :, past_keys:past_keys + LANES] = newt[LANES + g * NSA_HD:LANES + (g + 1) * NSA_HD, :]
    for c in range(ncomb):
        for rd in range(nsel):
            copy(bi, slot, c, rd).wait()

    qpt = qpt_ref[0]
    lane = lax.broadcasted_iota(jnp.int32, (1, LANES), 1)
    lane_p = lax.broadcasted_iota(jnp.int32, (1, page), 1)
    rep = NSA_HEADS // NSA_KV
    os_h = [jnp.zeros((NSA_HD, LANES), F32) for _ in range(NSA_HEADS)]
    for c in range(ncomb):
        g, t = divmod(c, n_dec)
        pieces = []
        for rd in range(nsel):
            j = info_ref[bi, c * nsel + rd]
            ok = info_ref[bi, ncomb * nsel + c * nsel + rd]
            chosen = (lane_p // SEL_BLOCK == j % (page // SEL_BLOCK)) & (ok > 0)
            pieces.append(jnp.where(chosen, 0.0, NEG))
        pieces.append(jnp.where(lane <= t, 0.0, NEG))
        bias = jnp.concatenate(pieces, axis=1)
        keys, vals = kvbuf[slot, c, 0], kvbuf[slot, c, 1]
        for r in range(rep):
            h = g * rep + r
            cidx = h * _SCOL + t
            qcol = qpt[g * NSA_HD:(g + 1) * NSA_HD, cidx:cidx + 1]
            s = jnp.sum(keys * qcol, axis=0, keepdims=True) + bias
            m = jnp.max(s, axis=1, keepdims=True)
            p = jnp.exp(s - m)
            l = jnp.sum(p, axis=1, keepdims=True)
            o = jnp.sum(vals * p, axis=1, keepdims=True) / jnp.maximum(l, 1e-30)
            os_h[h] = jnp.where(lane == t, o, os_h[h])
    oc, ow, ngt = oc_ref[0], ow_ref[0], ngt_ref[0]
    for h in range(NSA_HEADS):
        g = h // rep

        def to_front(a):
            blk = a[g * NSA_HD:(g + 1) * NSA_HD, :]
            return blk if h == 0 else pltpu.roll(blk, LANES - h * _SCOL, axis=1)

        y_ref[0, h * NSA_HD:(h + 1) * NSA_HD, :] = (ngt[3 * h:3 * h + 1, :] * to_front(oc)
                                                   + ngt[3 * h + 1:3 * h + 2, :] * os_h[h]
                                                   + ngt[3 * h + 2:3 * h + 3, :] * to_front(ow))


def _nsa_sample_b_call(page_table, info, cache_t, qpt, newt, oc, ow, ngt, n_dec):
    b = qpt.shape[0]
    page = cache_t.shape[2]
    cache_t = cache_t.reshape(cache_t.shape[0], 4, NSA_KV, NSA_HD, page)
    ncomb = NSA_KV * n_dec
    keys = (N_SEL - 1) * page + LANES
    spec = lambda a: pl.BlockSpec((1,) + a.shape[1:], lambda i, pt, sl: (i, 0, 0))
    grid_spec = pltpu.PrefetchScalarGridSpec(
        num_scalar_prefetch=2, grid=(b,),
        in_specs=[pl.BlockSpec(memory_space=pl.ANY), spec(qpt), spec(newt), spec(oc), spec(ow), spec(ngt)],
        out_specs=pl.BlockSpec((1, NSA_HEADS * NSA_HD, LANES), lambda i, pt, sl: (i, 0, 0)),
        scratch_shapes=[pltpu.VMEM((2, ncomb, 2, NSA_HD, keys), F32), pltpu.SemaphoreType.DMA((2,))])
    return pl.pallas_call(
        functools.partial(_nsa_sample_b_kernel, n_dec=n_dec, page=page), grid_spec=grid_spec,
        out_shape=jax.ShapeDtypeStruct((b, NSA_HEADS * NSA_HD, LANES), F32),
        compiler_params=_params(("arbitrary",)), name="nsa_sample_b",
    )(page_table, info, cache_t, qpt, newt, oc, ow, ngt)


def _merge_kernel(x_ref, g1_ref, wg_ref, ya_ref, yb_ref, yc_ref, yd_ref, wout_ref, wo_ref, o_ref):
    x = x_ref[...]
    h = _rms(x, g1_ref[...]).astype(BF16)
    mix = None
    for i, y_ref in enumerate((ya_ref, yb_ref, yc_ref, yd_ref)):
        gate = jax.nn.sigmoid(_nt(h, wg_ref[i * D_MODEL:(i + 1) * D_MODEL, :]))
        term = gate * _bdot(y_ref[...], wout_ref[i])
        mix = term if mix is None else mix + term
    o_ref[...] = x + _bdot(mix, wo_ref[...])


def _merge_call(x, ys, lw):
    n = x.shape[0]
    tm = min(512, n)
    row = lambda w: pl.BlockSpec((tm, w), lambda i: (i, 0))
    return pl.pallas_call(
        _merge_kernel, grid=(n // tm,),
        in_specs=[row(D_MODEL), _const_spec(lw['g1'].shape), _const_spec(lw['wgate'].shape)] + [row(2 * LANES)] * 4
        + [_const_spec(lw['wout'].shape), _const_spec(lw['wo'].shape)],
        out_specs=row(D_MODEL), out_shape=jax.ShapeDtypeStruct((n, D_MODEL), F32),
        compiler_params=_params(("parallel",)), name="merge",
    )(x, lw['g1'], lw['wgate'], *ys, lw['wout'], lw['wo'])


_FFN_CHUNK = 256


def _ffn_kernel(x_ref, g2_ref, wg_ref, wu_ref, wd_ref, o_ref):
    x = x_ref[...]
    h = _rms(x, g2_ref[...]).astype(BF16)
    acc = x
    for c in range(0, D_FF, _FFN_CHUNK):
        g = jnp.dot(h, wg_ref[:, c:c + _FFN_CHUNK], preferred_element_type=F32)
        u = jnp.dot(h, wu_ref[:, c:c + _FFN_CHUNK], preferred_element_type=F32)
        acc = acc + _bdot(g * jax.nn.sigmoid(g) * u, wd_ref[c:c + _FFN_CHUNK, :])
    o_ref[...] = acc


def _ffn_call(x, lw):
    n = x.shape[0]
    tm = min(512, n)
    row = pl.BlockSpec((tm, D_MODEL), lambda i: (i, 0))
    return pl.pallas_call(
        _ffn_kernel, grid=(n // tm,),
        in_specs=[row, _const_spec(lw['g2'].shape), _const_spec(lw['fg'].shape), _const_spec(lw['fu'].shape),
                  _const_spec(lw['fd'].shape)],
        out_specs=row, out_shape=jax.ShapeDtypeStruct((n, D_MODEL), F32),
        compiler_params=_params(("parallel",)), name="ffn",
    )(x, lw['g2'], lw['fg'], lw['fu'], lw['fd'])


def _block_diag(blocks):
    g, a, b = blocks.shape[-3:]
    eye = jnp.eye(g, dtype=blocks.dtype)
    out = jnp.einsum('...gab,gh->...gahb', blocks, eye)
    return out.reshape(blocks.shape[:-3] + (g * a, g * b))


def _seg_mean_matrix(width):
    seg = np.arange(width) // NSA_HD
    return jnp.asarray(np.where(seg[:, None] == seg[None, :], 1.0 / NSA_HD, 0.0), BF16)


def _band_matrix(nbp, ncp):
    ratio = SEL_BLOCK // CMP_STRIDE
    j = np.arange(nbp)[:, None]
    n = np.arange(ncp)[None, :]
    return jnp.asarray((n >= ratio * j - 1) & (n <= ratio * j + ratio - 1), BF16)


def _expand_matrix(nrows, nblk):
    return jnp.asarray(np.arange(nrows)[:, None] // SEL_BLOCK == np.arange(nblk)[None, :], BF16)


def _prep_layer(w, l, t_prompt, past_len):
    row = lambda a: a.reshape(1, -1).astype(F32)
    w_in_t = jnp.transpose(w['w_in'], (2, 0, 1))[:, l, :]
    pad_to = lambda a, n: jnp.pad(a, ((0, n - a.shape[0]), (0, 0)))
    wp = jnp.concatenate([w_in_t[0:1280], pad_to(w_in_t[1280:1296], LANES), w_in_t[1296:2576],
                          pad_to(w_in_t[2576:2588], LANES)], axis=0).astype(BF16)
    phi, pe = w['nsa_phi'][l], w['nsa_pe'][l]
    half = CMP_STRIDE

    def cmp_w(p):
        bd = lambda a: _block_diag(jnp.broadcast_to(a[:, None], (half, NSA_KV) + a.shape[1:]))
        per_l = jnp.concatenate([bd(p[:half]), bd(p[half:])], axis=-1)
        return per_l.reshape(half // 2, 2 * LANES, 2 * LANES).astype(BF16)

    def cmp_pe(p):
        return jnp.concatenate([jnp.tile(p[:half], (1, NSA_KV)), jnp.tile(p[half:], (1, NSA_KV))], axis=1).astype(F32)

    kw = GLA_HEADS * GLA_DK
    hk = np.arange(kw) // GLA_DK
    hv = np.arange(GLA_HEADS * GLA_DV) // GLA_DV
    same_head = hk[:, None] == hv[None, :]
    return {
        'g1': row(w['norm1'][l]), 'g2': row(w['norm2'][l]), 'wp': wp, 'wgate': w_in_t[2588:].astype(BF16),
        'wa2': jnp.pad(w['gla_wa2'][l], ((0, LANES - GLA_LOWRANK), (0, 0))).astype(BF16), 'ba': row(w['gla_ba'][l]),
        'qn': row(jnp.tile(w['nsa_qn'][l], NSA_HEADS)) * (NSA_HD ** -0.5),
        'kn0': row(jnp.tile(w['nsa_kn'][l][0], NSA_KV)), 'kn1': row(jnp.tile(w['nsa_kn'][l][1], NSA_KV)),
        'kn2': row(jnp.tile(w['nsa_kn'][l][2], NSA_KV)),
        'gb': row(jnp.pad(w['nsa_gb'][l], (0, LANES - 3 * NSA_HEADS))),
        'e256': _seg_mean_matrix(2 * LANES), 'e128': _seg_mean_matrix(LANES),
        'cw': jnp.pad(w['conv_w'][l], ((0, 1), (0, 0))).astype(F32), 'cb': row(w['conv_b'][l]),
        'clg': row(w['conv_ln_g'][l]), 'clb': row(w['conv_ln_b'][l]),
        'pw': _block_diag(w['pool_w'][l]).astype(BF16), 'psc': row(w['pool_scale'][l]),
        'on': row(jnp.tile(w['gla_onorm'][l], GLA_HEADS)),
        'eh': jnp.asarray(same_head, BF16), 'mbd': jnp.asarray(same_head.T, F32),
        'wk': cmp_w(phi[0]), 'wv': cmp_w(phi[1]), 'pek': cmp_pe(pe[0]), 'pev': cmp_pe(pe[1]),
        'mbt_p': _band_matrix(t_prompt // SEL_BLOCK, t_prompt // CMP_STRIDE),
        'mbt_s': _band_matrix(past_len // SEL_BLOCK, past_len // CMP_STRIDE),
        'e16': _expand_matrix(SEL_TILE, LANES),
        'wout': jnp.stack([w['w_out_conv'][l], w['w_out_gla'][l], w['w_out_pool'][l], w['w_out_nsa'][l]]).astype(BF16),
        'wo': w['w_o'][l].astype(BF16),
        'fg': w['ffn_gate'][l].astype(BF16), 'fu': w['ffn_up'][l].astype(BF16), 'fd': w['ffn_down'][l].astype(BF16),
    }


def _prompt_layer(x, lw):
    b, t, _ = x.shape
    xf = x.reshape(b * t, D_MODEL)
    a, qk, v, og, la, up, q, rows, win, ng, ksb, svt, kwb, wvt = _proj_call(xf, lw, True)
    r3 = lambda z: z.reshape(b, t, z.shape[-1])
    ya, u = _conv_call(r3(a), jnp.zeros((b, CONV_WIDTH - 1, CONV_CH), F32), lw)
    yc = _pool_call(r3(up), jnp.zeros((b, POOL_STATE, POOL_CH), F32), 0, lw)
    yb, gla_state = _gla_call(r3(qk), r3(v), r3(la), r3(og), jnp.zeros((b, GLA_HEADS, GLA_DK, GLA_DV), F32), lw)
    page = 2 * SEL_BLOCK
    cache = rows.reshape(b * t // page, page, rows.shape[-1])
    table = jnp.arange(b * t // page, dtype=jnp.int32).reshape(b, t // page)
    kch, kcl, vct = _compress_call(cache, table, lw, False)
    tcols =lambda z: z.reshape(LANES, b, t).swapaxes(0, 1)
    yd = _nsa_prompt_call(r3(q), r3(ng), kch, kcl, vct, r3(ksb), tcols(svt), r3(kwb), tcols(wvt), lw)
    x1 = _merge_call(xf, [ya.reshape(b * t, -1), yb.reshape(b * t, -1), yc.reshape(b * t, -1), yd.reshape(b * t, -1)], lw)
    x2 = _ffn_call(x1, lw).reshape(b, t, D_MODEL)
    keep = min(WINDOW, t)
    return (x2, r3(rows).reshape(b, t, 4, NSA_KV, NSA_HD), r3(win)[:, -keep:].reshape(b, keep, 2, NSA_KV, NSA_HD),
            u[:, -(CONV_WIDTH - 1):], r3(up)[:, -POOL_STATE:], gla_state)


def _sample_layer(x, lw, cache_t, cwin, conv_prev, pool_prev, gla_prev, page_table):
    b, t, _ = x.shape
    page = cache_t.shape[2]
    past_len = page_table.shape[1] * page
    assert past_len % SEL_BLOCK == 0 and t <= _SCOL and page % SEL_BLOCK == 0
    xf = x.reshape(b * t, D_MODEL)
    a, qk, v, og, la, up, q, rows, win, ng = _proj_call(xf, lw, False)
    r3 = lambda z: z.reshape(b, t, z.shape[-1])
    ya, u = _conv_call(r3(a), conv_prev, lw)
    yc = _pool_call(r3(up), pool_prev, past_len, lw)
    yb, gla_state = _gla_call(r3(qk), r3(v), r3(la), r3(og), gla_prev, lw)
    kch, kcl, vct = _compress_call(cache_t, page_table, lw, True)
    q4 = jnp.pad(r3(q).reshape(b, t, NSA_HEADS, NSA_HD), ((0, 0), (0, _SCOL - t), (0, 0), (0, 0))).swapaxes(1, 2)
    grp = jnp.arange(NSA_HEADS) // (NSA_HEADS // NSA_KV)
    onehot = (grp[:, None] == jnp.arange(NSA_KV)[None, :]).astype(F32)
    qp = jnp.einsum('bhtd,hg->bhtgd', q4, onehot).reshape(b, NSA_HEADS * _SCOL, LANES)
    qp = jnp.pad(qp, ((0, 0), (0, LANES - NSA_HEADS * _SCOL), (0, 0)))
    pad_rows = lambda z, n: jnp.pad(z, ((0, 0), (0, n - z.shape[1]), (0, 0)))
    cwin2 = cwin.reshape(b, cwin.shape[1], -1)
    oc, ow, idx, ok = _nsa_sample_a_call(qp, kch, kcl, vct, cwin2, pad_rows(r3(win), LANES), lw, past_len, t)
    cols = (jnp.arange(NSA_KV)[:, None] * (NSA_HEADS // NSA_KV) * _SCOL + jnp.arange(t)[None, :]).reshape(-1)
    per_comb = lambda z: z[:, :N_SEL - 1, :][:, :, cols].swapaxes(1, 2).reshape(b, -1)
    info = jnp.concatenate([per_comb(idx), per_comb(ok).astype(jnp.int32)], axis=1)
    lanes_t = lambda z, n: jnp.pad(z.swapaxes(1, 2), ((0, 0), (0, n - z.shape[2]), (0, LANES - z.shape[1])))
    yt = _nsa_sample_b_call(page_table, info, cache_t, qp.swapaxes(1, 2), lanes_t(r3(rows)[:, :, 2 * LANES:], 2 * LANES),
                            oc, ow, lanes_t(r3(ng)[:, :, :3 * NSA_HEADS], 2 * SUBLANES), t)
    yd = yt[:, :, :t].swapaxes(1, 2)
    x1 = _merge_call(xf, [ya.reshape(b * t, -1), yb.reshape(b * t, -1), yc.reshape(b * t, -1), yd.reshape(b * t, -1)], lw)
    x2 = _ffn_call(x1, lw).reshape(b, t, D_MODEL)
    win_all = jnp.concatenate([cwin2, r3(win)], axis=1)
    keep = min(WINDOW, win_all.shape[1])
    return (x2, r3(rows).reshape(b, t, 4, NSA_KV, NSA_HD), win_all[:, -keep:].reshape(b, keep, 2, NSA_KV, NSA_HD),
            jnp.concatenate([conv_prev, u], axis=1)[:, -(CONV_WIDTH - 1):],
            jnp.concatenate([pool_prev, r3(up)], axis=1)[:, -POOL_STATE:], gla_state)


def kernel(x_prompt, x_sample, cache_nsa_kv, cache_win_kv, state_conv, state_pool, state_gla, page_table, norm1, w_in, conv_w, conv_b, conv_ln_g, conv_ln_b, w_out_conv, gla_wa2, gla_ba, gla_onorm, w_out_gla, pool_w, pool_scale, w_out_pool, nsa_qn, nsa_kn, nsa_pe, nsa_phi, nsa_gb, w_out_nsa, w_o, norm2, ffn_gate, ffn_up, ffn_down):
    w = dict(norm1=norm1, w_in=w_in, conv_w=conv_w, conv_b=conv_b, conv_ln_g=conv_ln_g, conv_ln_b=conv_ln_b,
             w_out_conv=w_out_conv, gla_wa2=gla_wa2, gla_ba=gla_ba, gla_onorm=gla_onorm, w_out_gla=w_out_gla,
             pool_w=pool_w, pool_scale=pool_scale, w_out_pool=w_out_pool, nsa_qn=nsa_qn, nsa_kn=nsa_kn,
             nsa_pe=nsa_pe, nsa_phi=nsa_phi, nsa_gb=nsa_gb, w_out_nsa=w_out_nsa, w_o=w_o, norm2=norm2,
             ffn_gate=ffn_gate, ffn_up=ffn_up, ffn_down=ffn_down)
    depth, n_pool, page = cache_nsa_kv.shape[:3]
    past_len = page_table.shape[1] * page
    cache = jnp.moveaxis(cache_nsa_kv, 2, -1).reshape(depth * n_pool, -1, page)
    xp, xs = x_prompt, x_sample
    outs_p, outs_s = [], []
    for l in range(depth):
        lw = _prep_layer(w, l, xp.shape[1], past_len)
        xp, *st_p = _prompt_layer(xp, lw)
        xs, *st_s = _sample_layer(xs, lw, cache, cache_win_kv[l], state_conv[l], state_pool[l],
                                  state_gla[l], page_table + l * n_pool)
        outs_p.append(st_p)
        outs_s.append(st_s)
    stack = lambda outs, i: jnp.stack([o[i] for o in outs])
    return (xp, xs) + tuple(stack(outs_p, i) for i in range(5)) + tuple(stack(outs_s, i) for i in range(5))
```
